```python
import jax, jax.numpy as jnp
from jax import lax
import numpy as np

D_MODEL = 1024
BATCH = 8
SEQ = 2048
DEPTH = 4
DEC_BATCH = 2
DEC_SEQ = 16384
PAST_LEN = 128

LRU_WIDTH = 1024
LRU_BLOCKS = 16
LRU_BLOCK = LRU_WIDTH // LRU_BLOCKS
LRU_C = 8.0
CONV_K = 4
SSD_INNER = 2 * D_MODEL
SSD_HEAD_DIM = 64
SSD_HEADS = SSD_INNER // SSD_HEAD_DIM
SSD_GROUPS = 4
SSD_HPG = SSD_HEADS // SSD_GROUPS
SSD_STATE = 128
SSD_CHUNK = 128
SSD_XBC = SSD_INNER + 2 * SSD_GROUPS * SSD_STATE
N_EXPERTS = 16
EXPERT_FF = 2048
CAPACITY_FACTOR = 2
NORM_EPS = 1e-6
IN_SIZES = (LRU_WIDTH, LRU_WIDTH, SSD_INNER, SSD_XBC, 2 * SSD_HEADS, 2 * D_MODEL)
N_IN = LRU_WIDTH * 2 + SSD_INNER + SSD_XBC + 2 * SSD_HEADS + 2 * D_MODEL

kernel_name = "hybrid_bidir_rglru_ssd_ec_moe_encoder"


def rmsnorm(x, g):
    xf = x.astype(jnp.float32)
    y = xf * lax.rsqrt(jnp.mean(xf * xf, axis=-1, keepdims=True) + NORM_EPS) * g.astype(jnp.float32)
    return y.astype(x.dtype)


def centred_dwconv(x, w, b):
    ch = x.shape[-1]
    left = CONV_K // 2
    right = CONV_K - 1 - left
    y = lax.conv_general_dilated(x, w[:, None, :].astype(x.dtype), window_strides=(1,),
                                 padding=[(left, right)], dimension_numbers=('NWC', 'WIO', 'NWC'),
                                 feature_group_count=ch)
    return y + b.astype(x.dtype)


def block_diag_linear(x, w, b):
    xb = x.reshape(x.shape[:-1] + (LRU_BLOCKS, LRU_BLOCK))
    return jnp.einsum('blnk,nkj->blnj', xb, w.astype(x.dtype)).reshape(x.shape) + b.astype(x.dtype)


def _lin_combine(left, right):
    a1, b1 = left
    a2, b2 = right
    return a1 * a2, a2 * b1 + b2


def rg_lru_direction(u, wa, ba, wx, bx, lam, reverse):
    r = jax.nn.sigmoid(block_diag_linear(u, wa, ba))
    i = jax.nn.sigmoid(block_diag_linear(u, wx, bx))
    log_a = LRU_C * r * jax.nn.log_sigmoid(lam.astype(jnp.float32))
    a = jnp.exp(log_a)
    bterm = jnp.sqrt(-jnp.expm1(2.0 * log_a)) * (i * u)
    _, h = lax.associative_scan(_lin_combine, (a, bterm), axis=1, reverse=reverse)
    return h


def ssd_chunked(x, dt, A, B, C):
    bsz, seqlen = x.shape[0], x.shape[1]
    nc = seqlen // SSD_CHUNK
    q = SSD_CHUNK
    x = x.reshape(bsz, nc, q, SSD_GROUPS, SSD_HPG, SSD_HEAD_DIM)
    dt = dt.reshape(bsz, nc, q, SSD_GROUPS, SSD_HPG)
    B = B.reshape(bsz, nc, q, SSD_GROUPS, SSD_STATE)
    C = C.reshape(bsz, nc, q, SSD_GROUPS, SSD_STATE)
    acs = jnp.cumsum(dt * A.reshape(SSD_GROUPS, SSD_HPG), axis=2)
    xdt = x * dt[..., None]
    seg = acs[:, :, :, None] - acs[:, :, None, :]
    mask = jnp.tril(jnp.ones((q, q), dtype=bool))[None, None, :, :, None, None]
    lmat = jnp.exp(jnp.where(mask, seg, -jnp.inf))
    cb = jnp.einsum('bclgn,bcsgn->bclsg', C, B)
    y_diag = jnp.einsum('bclsg,bclsge,bcsgep->bclgep', cb, lmat, xdt)
    decay_states = jnp.exp(acs[:, :, -1:] - acs)
    states = jnp.einsum('bcsgn,bcsge,bcsgep->bcgepn', B, decay_states, xdt)
    chunk_decay = jnp.exp(acs[:, :, -1])

    def step(carry, inp):
        st, dec = inp
        return carry * dec[..., None, None] + st, carry

    init = jnp.zeros((bsz, SSD_GROUPS, SSD_HPG, SSD_HEAD_DIM, SSD_STATE), jnp.float32)
    _, prev = lax.scan(step, init, (jnp.moveaxis(states, 1, 0), jnp.moveaxis(chunk_decay, 1, 0)))
    prev = jnp.moveaxis(prev, 0, 1)
    y_off = jnp.einsum('bclgn,bcgepn,bclge->bclgep', C, prev, jnp.exp(acs))
    return (y_diag + y_off).reshape(bsz, seqlen, SSD_HEADS, SSD_HEAD_DIM)


def mixer(h, w_in, lru_conv_w, lru_conv_b, lru_wa, lru_ba, lru_wx, lru_bx, lru_lambda,
          ssd_conv_w, ssd_conv_b, ssd_dt_bias, ssd_a_log, ssd_d, ssd_norm, w_branch, w_out):
    bsz, seqlen, _ = h.shape
    proj = h @ w_in.astype(h.dtype)
    cuts = [int(c) for c in np.cumsum(IN_SIZES)[:-1]]
    lru_x, lru_gate, ssd_z, ssd_xbc, ssd_dt, gate_raw = jnp.split(proj, cuts, axis=-1)

    u = centred_dwconv(lru_x, lru_conv_w, lru_conv_b).astype(jnp.float32)
    h_lru = (rg_lru_direction(u, lru_wa[0], lru_ba[0], lru_wx[0], lru_bx[0], lru_lambda[0], False)
             + rg_lru_direction(u, lru_wa[1], lru_ba[1], lru_wx[1], lru_bx[1], lru_lambda[1], True))
    y_a = h_lru * jax.nn.gelu(lru_gate.astype(jnp.float32))

    xbc = jax.nn.silu(centred_dwconv(ssd_xbc, ssd_conv_w, ssd_conv_b).astype(jnp.float32))
    xs, bm, cm = jnp.split(xbc, [SSD_INNER, SSD_INNER + SSD_GROUPS * SSD_STATE], axis=-1)
    xs = xs.reshape(bsz, seqlen, SSD_HEADS, SSD_HEAD_DIM)
    bm = bm.reshape(bsz, seqlen, SSD_GROUPS, SSD_STATE)
    cm = cm.reshape(bsz, seqlen, SSD_GROUPS, SSD_STATE)
    dt_raw = ssd_dt.astype(jnp.float32).reshape(bsz, seqlen, 2, SSD_HEADS)
    dt_f = jax.nn.softplus(dt_raw[:, :, 0] + ssd_dt_bias[0].astype(jnp.float32))
    dt_b = jax.nn.softplus(dt_raw[:, :, 1] + ssd_dt_bias[1].astype(jnp.float32))
    a_f = -jnp.exp(ssd_a_log[0].astype(jnp.float32))
    a_b = -jnp.exp(ssd_a_log[1].astype(jnp.float32))
    y_f = ssd_chunked(xs, dt_f, a_f, bm, cm)
    flip = lambda t: jnp.flip(t, axis=1)
    y_bw = flip(ssd_chunked(flip(xs), flip(dt_b), a_b, flip(bm), flip(cm)))
    y_s = y_f + y_bw + ssd_d.astype(jnp.float32)[:, None] * xs
    y_s = y_s.reshape(bsz, seqlen, SSD_INNER) * jax.nn.silu(ssd_z.astype(jnp.float32))
    y_b = rmsnorm(y_s, ssd_norm)

    wb = w_branch.astype(jnp.float32)
    p_a = y_a @ wb[:LRU_WIDTH]
    p_b = y_b @ wb[LRU_WIDTH:]
    gates = jax.nn.sigmoid(gate_raw.astype(jnp.float32))
    merged = gates[..., :D_MODEL] * p_a + gates[..., D_MODEL:] * p_b
    return (merged @ w_out.astype(jnp.float32)).astype(h.dtype)


def expert_choice_ffn(x, w_router, w_gate, w_up, w_down):
    bsz, seqlen, d = x.shape
    n_tok = bsz * seqlen
    cap = CAPACITY_FACTOR * n_tok // N_EXPERTS
    xt = x.reshape(n_tok, d)
    logits = (xt @ w_router.astype(x.dtype)).astype(jnp.float32)
    aff = jax.nn.softmax(logits, axis=-1)
    w_sel, idx = lax.top_k(aff.T, cap)
    xe = xt[idx]
    hid = (jax.nn.silu(jnp.einsum('ecd,edf->ecf', xe, w_gate.astype(x.dtype)))
           * jnp.einsum('ecd,edf->ecf', xe, w_up.astype(x.dtype)))
    ye = jnp.einsum('ecf,efd->ecd', hid, w_down.astype(x.dtype)) * w_sel[..., None].astype(x.dtype)
    y = jnp.zeros_like(xt).at[idx.reshape(-1)].add(ye.reshape(-1, d))
    return y.reshape(bsz, seqlen, d)


def trunk(x, params):
    (norm_mix, w_in, lru_conv_w, lru_conv_b, lru_wa, lru_ba, lru_wx, lru_bx, lru_lambda,
     ssd_conv_w, ssd_conv_b, ssd_dt_bias, ssd_a_log, ssd_d, ssd_norm, w_branch, w_out,
     norm_ffn, w_router, w_gate, w_up, w_down, norm_final) = params
    for i in range(DEPTH):
        h = rmsnorm(x, norm_mix[i])
        x = x + mixer(h, w_in[i], lru_conv_w[i], lru_conv_b[i], lru_wa[i], lru_ba[i], lru_wx[i],
                      lru_bx[i], lru_lambda[i], ssd_conv_w[i], ssd_conv_b[i], ssd_dt_bias[i],
                      ssd_a_log[i], ssd_d[i], ssd_norm[i], w_branch[i], w_out[i])
        h = rmsnorm(x, norm_ffn[i])
        x = x + expert_choice_ffn(h, w_router[i], w_gate[i], w_up[i], w_down[i])
    return rmsnorm(x, norm_final)


def setup_inputs(seed: int = 0) -> dict:
    key = jax.random.key(seed)
    ks = jax.random.split(key, 32)
    f32 = jnp.float32
    nrm = lambda k, shape, s: jax.random.normal(k, shape, f32) * s
    gain = lambda k, shape: 1.0 + 0.02 * jax.random.normal(k, shape, f32)
    u = jax.random.uniform(ks[10], (DEPTH, 2, LRU_WIDTH), f32, 0.9, 0.999)
    a0 = u ** (1.0 / LRU_C)
    lru_lambda = jnp.log(a0) - jnp.log1p(-a0)
    dt0 = jnp.exp(jax.random.uniform(ks[14], (DEPTH, 2, SSD_HEADS), f32, np.log(1e-3), np.log(1e-1)))
    ssd_dt_bias = dt0 + jnp.log(-jnp.expm1(-dt0))
    ssd_a_log = jnp.log(jax.random.uniform(ks[15], (DEPTH, 2, SSD_HEADS), f32, 1.0, 16.0))
    w_branch = jnp.concatenate([nrm(ks[19], (DEPTH, LRU_WIDTH, D_MODEL), LRU_WIDTH ** -0.5),
                                nrm(ks[20], (DEPTH, SSD_INNER, D_MODEL), SSD_INNER ** -0.5)], axis=1)
    return {
        "x_prompt": jax.random.normal(ks[0], (BATCH, SEQ, D_MODEL), f32),
        "x_sample": jax.random.normal(ks[1], (DEC_BATCH, DEC_SEQ, D_MODEL), f32),
        "norm_mix": gain(ks[2], (DEPTH, D_MODEL)),
        "w_in": nrm(ks[3], (DEPTH, D_MODEL, N_IN), D_MODEL ** -0.5),
        "lru_conv_w": nrm(ks[4], (DEPTH, CONV_K, LRU_WIDTH), CONV_K ** -0.5),
        "lru_conv_b": nrm(ks[5], (DEPTH, LRU_WIDTH), 0.01),
        "lru_wa": nrm(ks[6], (DEPTH, 2, LRU_BLOCKS, LRU_BLOCK, LRU_BLOCK), LRU_BLOCK ** -0.5),
        "lru_ba": nrm(ks[7], (DEPTH, 2, LRU_WIDTH), 0.01),
        "lru_wx": nrm(ks[8], (DEPTH, 2, LRU_BLOCKS, LRU_BLOCK, LRU_BLOCK), LRU_BLOCK ** -0.5),
        "lru_bx": nrm(ks[9], (DEPTH, 2, LRU_WIDTH), 0.01),
        "lru_lambda": lru_lambda,
        "ssd_conv_w": nrm(ks[11], (DEPTH, CONV_K, SSD_XBC), CONV_K ** -0.5),
        "ssd_conv_b": nrm(ks[12], (DEPTH, SSD_XBC), 0.01),
        "ssd_dt_bias": ssd_dt_bias,
        "ssd_a_log": ssd_a_log,
        "ssd_d": 1.0 + 0.1 * jax.random.normal(ks[16], (DEPTH, SSD_HEADS), f32),
        "ssd_norm": gain(ks[17], (DEPTH, SSD_INNER)),
        "w_branch": w_branch,
        "w_out": nrm(ks[21], (DEPTH, D_MODEL, D_MODEL), D_MODEL ** -0.5),
        "norm_ffn": gain(ks[22], (DEPTH, D_MODEL)),
        "w_router": nrm(ks[23], (DEPTH, D_MODEL, N_EXPERTS), D_MODEL ** -0.5),
        "w_gate": nrm(ks[24], (DEPTH, N_EXPERTS, D_MODEL, EXPERT_FF), D_MODEL ** -0.5),
        "w_up": nrm(ks[25], (DEPTH, N_EXPERTS, D_MODEL, EXPERT_FF), D_MODEL ** -0.5),
        "w_down": nrm(ks[26], (DEPTH, N_EXPERTS, EXPERT_FF, D_MODEL), EXPERT_FF ** -0.5),
        "norm_final": gain(ks[27], (D_MODEL,)),
    }


def reference(x_prompt, x_sample, norm_mix, w_in, lru_conv_w, lru_conv_b, lru_wa, lru_ba, lru_wx,
              lru_bx, lru_lambda, ssd_conv_w, ssd_conv_b, ssd_dt_bias, ssd_a_log, ssd_d, ssd_norm,
              w_branch, w_out, norm_ffn, w_router, w_gate, w_up, w_down, norm_final):
    params = (norm_mix, w_in, lru_conv_w, lru_conv_b, lru_wa, lru_ba, lru_wx, lru_bx, lru_lambda,
              ssd_conv_w, ssd_conv_b, ssd_dt_bias, ssd_a_log, ssd_d, ssd_norm, w_branch, w_out,
              norm_ffn, w_router, w_gate, w_up, w_down, norm_final)
    y_prompt = trunk(x_prompt, params)
    y_sample = trunk(x_sample, params)
    return (y_prompt, y_sample)
```

```python
import functools

import jax
import jax.numpy as jnp
from jax import lax
from jax.experimental import pallas as pl
from jax.experimental.pallas import tpu as pltpu

f32 = jnp.float32
bf16 = jnp.bfloat16
i32 = jnp.int32

D_MODEL = 1024
DEPTH = 4
LRU_WIDTH = 1024
LRU_BLOCKS = 16
LRU_BLOCK = LRU_WIDTH // LRU_BLOCKS
LRU_C = 8.0
CONV_K = 4
SSD_INNER = 2 * D_MODEL
SSD_HEAD_DIM = 64
SSD_HEADS = SSD_INNER // SSD_HEAD_DIM
SSD_GROUPS = 4
SSD_STATE = 128
SSD_XBC = SSD_INNER + 2 * SSD_GROUPS * SSD_STATE
N_EXPERTS = 16
EXPERT_FF = 2048
CAPACITY_FACTOR = 2
NORM_EPS = 1e-6

LANES = 128
BF16_ROWS = 16
VMEM_LIMIT = 56 * 1024 * 1024

TM1 = 512
TN1 = 512
HALO = BF16_ROWS
N_MAIN = 2 * LRU_WIDTH + SSD_INNER + SSD_XBC + 2 * D_MODEL
NJ1 = N_MAIN // TN1
N_PAIR = SSD_HEADS // 2
DT_ROWS = 8 * N_PAIR

SCAN_ROWS = 128
CHUNK = 128
TOK_TILE = 128
SLAB = TOK_TILE + BF16_ROWS
TM5 = 256
TMF = 1024
TFF = 512
PFX = 512
NEG = -1e30


def _cparams(sem, vmem=None):
    return pltpu.CompilerParams(dimension_semantics=sem, vmem_limit_bytes=vmem)


def _inproj_kernel(l_ref, x_ref, xp_ref, xn_ref, g_ref, w_ref, cw_ref, cb_ref, wdt_ref, bdt_ref,
                   u_ref, lg_ref, z_ref, xbc_ref, gr_ref, dt_ref, h_scr, a_scr, *, tiles_per_seq):
    i = pl.program_id(0)
    j = pl.program_id(1)

    @pl.when(j == 0)
    def _():
        g = g_ref[...]

        def nrm(x):
            return x * lax.rsqrt(jnp.mean(x * x, axis=-1, keepdims=True) + NORM_EPS) * g

        it = i % tiles_per_seq
        top_ok = (it != 0).astype(f32)
        bot_ok = (it != tiles_per_seq - 1).astype(f32)
        h_scr[0:HALO, :] = (nrm(xp_ref[...]) * top_ok).astype(bf16)
        hm = nrm(x_ref[...]).astype(bf16)
        h_scr[HALO:HALO + TM1, :] = hm
        h_scr[HALO + TM1:, :] = (nrm(xn_ref[...]) * bot_ok).astype(bf16)
        raw = lax.dot_general(wdt_ref[...], hm, (((1,), (1,)), ((), ())), preferred_element_type=f32)
        dt_ref[...] = jax.nn.softplus(raw + bdt_ref[...])

    a_scr[...] = jnp.dot(h_scr[...], w_ref[...], preferred_element_type=f32)

    def conv():
        cw = cw_ref[...]
        out = cb_ref[...] + cw[0:1] * a_scr[HALO - 2:HALO - 2 + TM1, :]
        out = out + cw[1:2] * a_scr[HALO - 1:HALO - 1 + TM1, :]
        out = out + cw[2:3] * a_scr[HALO:HALO + TM1, :]
        out = out + cw[3:4] * a_scr[HALO + 1:HALO + 1 + TM1, :]
        return out

    @pl.when(j < 2)
    def _():
        u_ref[...] = conv().astype(bf16)

    @pl.when((j >= 2) & (j < 4))
    def _():
        lg_ref[...] = a_scr[HALO:HALO + TM1, :].astype(bf16)

    @pl.when((j >= 4) & (j < 8))
    def _():
        z_ref[...] = a_scr[HALO:HALO + TM1, :].astype(bf16)

    @pl.when((j >= 8) & (j < 14))
    def _():
        v = conv()
        xbc_ref[...] = (v * jax.nn.sigmoid(v)).astype(bf16)

    @pl.when(j >= 14)
    def _():
        gr_ref[...] = a_scr[HALO:HALO + TM1, :].astype(bf16)


def _inproj(lidx, x, p, seq_len):
    t = x.shape[0]
    nrow = t // TM1
    hb = TM1 // HALO
    nhb = t // HALO
    kern = functools.partial(_inproj_kernel, tiles_per_seq=seq_len // TM1)

    def seg(lo, hi):
        return lambda i, j, l: (i, jnp.clip(j, lo, hi) - lo)

    grid_spec = pltpu.PrefetchScalarGridSpec(
        num_scalar_prefetch=1,
        grid=(nrow, NJ1),
        in_specs=[
            pl.BlockSpec((TM1, D_MODEL), lambda i, j, l: (i, 0)),
            pl.BlockSpec((HALO, D_MODEL), lambda i, j, l: (jnp.maximum(i * hb - 1, 0), 0)),
            pl.BlockSpec((HALO, D_MODEL), lambda i, j, l: (jnp.minimum((i + 1) * hb, nhb - 1), 0)),
            pl.BlockSpec((None, 1, D_MODEL), lambda i, j, l: (l[0], 0, 0)),
            pl.BlockSpec((None, D_MODEL, TN1), lambda i, j, l: (l[0], 0, j)),
            pl.BlockSpec((None, CONV_K, TN1), lambda i, j, l: (l[0], 0, j)),
            pl.BlockSpec((None, 1, TN1), lambda i, j, l: (l[0], 0, j)),
            pl.BlockSpec((None, DT_ROWS, D_MODEL), lambda i, j, l: (l[0], 0, 0)),
            pl.BlockSpec((None, DT_ROWS, 1), lambda i, j, l: (l[0], 0, 0)),
        ],
        out_specs=[
            pl.BlockSpec((TM1, TN1), seg(0, 1)),
            pl.BlockSpec((TM1, TN1), seg(2, 3)),
            pl.BlockSpec((TM1, TN1), seg(4, 7)),
            pl.BlockSpec((TM1, TN1), seg(8, 13)),
            pl.BlockSpec((TM1, TN1), seg(14, 17)),
            pl.BlockSpec((DT_ROWS, TM1), lambda i, j, l: (0, i)),
        ],
        scratch_shapes=[pltpu.VMEM((TM1 + 2 * HALO, D_MODEL), bf16),
                        pltpu.VMEM((TM1 + 2 * HALO, TN1), f32)],
    )
    return pl.pallas_call(
        kern,
        grid_spec=grid_spec,
        out_shape=[
            jax.ShapeDtypeStruct((t, LRU_WIDTH), bf16),
            jax.ShapeDtypeStruct((t, LRU_WIDTH), bf16),
            jax.ShapeDtypeStruct((t, SSD_INNER), bf16),
            jax.ShapeDtypeStruct((t, SSD_XBC), bf16),
            jax.ShapeDtypeStruct((t, 2 * D_MODEL), bf16),
            jax.ShapeDtypeStruct((DT_ROWS, t), f32),
        ],
        compiler_params=_cparams(("parallel", "arbitrary"), 40 * 1024 * 1024),
        name="inproj",
    )(lidx, x, x, x, p["norm_mix"], p["w_main"], p["conv_w"], p["conv_b"], p["w_dt"], p["b_dt"])


def _lru_kernel(l_ref, u_ref, g_ref, w_ref, b_ref, lam_ref, o_ref, hf_scr, *, seq_len):
    r = SCAN_ROWS
    nsub = seq_len // r
    ngrp = r // 8
    c8 = LRU_C * jax.nn.log_sigmoid(lam_ref[...])
    row = lax.broadcasted_iota(i32, (r, LANES), 0) & 7

    def gates(j, d):
        off = pl.multiple_of(j * r, r)
        u = u_ref[pl.ds(off, r), :]
        g = jnp.dot(u, w_ref[:, d * 2 * LANES:(d + 1) * 2 * LANES], preferred_element_type=f32)
        g = g + b_ref[:, d * 2 * LANES:(d + 1) * 2 * LANES]
        rg = jax.nn.sigmoid(g[:, :LANES])
        ig = jax.nn.sigmoid(g[:, LANES:])
        a = jnp.exp(c8[d:d + 1] * rg)
        b = jnp.sqrt(1.0 - a * a) * (ig * u.astype(f32))
        return off, a, b

    def scan8(a, b, reverse):
        for d in (1, 2, 4):
            if reverse:
                m = row <= 7 - d
                sh = r - d
            else:
                m = row >= d
                sh = d
            a_s = jnp.where(m, pltpu.roll(a, sh, 0), 1.0)
            b_s = jnp.where(m, pltpu.roll(b, sh, 0), 0.0)
            b = b + a * b_s
            a = a * a_s
        return a, b

    def carry(a, b, c, reverse):
        hs = [None] * ngrp
        order = range(ngrp - 1, -1, -1) if reverse else range(ngrp)
        for k in order:
            h = b[8 * k:8 * k + 8] + a[8 * k:8 * k + 8] * c
            c = h[0:1] if reverse else h[7:8]
            hs[k] = h
        return jnp.concatenate(hs, axis=0), c

    def fwd(j, c):
        off, a, b = gates(j, 0)
        a, b = scan8(a, b, False)
        h, c = carry(a, b, c, False)
        hf_scr[pl.ds(off, r), :] = h
        return c

    lax.fori_loop(0, nsub, fwd, jnp.zeros((1, LANES), f32))

    def bwd(jj, c):
        off, a, b = gates(nsub - 1 - jj, 1)
        a, b = scan8(a, b, True)
        h, c = carry(a, b, c, True)
        gate = jax.nn.gelu(g_ref[pl.ds(off, r), :].astype(f32))
        o_ref[pl.ds(off, r), :] = ((hf_scr[pl.ds(off, r), :] + h) * gate).astype(bf16)
        return c

    lax.fori_loop(0, nsub, bwd, jnp.zeros((1, LANES), f32))


def _lru(lidx, u, lg, p, seq_len):
    t = u.shape[0]
    nb = t // seq_len
    ncb = LRU_WIDTH // LANES
    kern = functools.partial(_lru_kernel, seq_len=seq_len)
    grid_spec = pltpu.PrefetchScalarGridSpec(
        num_scalar_prefetch=1,
        grid=(nb, ncb),
        in_specs=[
            pl.BlockSpec((seq_len, LANES), lambda b, c, l: (b, c)),
            pl.BlockSpec((seq_len, LANES), lambda b, c, l: (b, c)),
            pl.BlockSpec((None, None, LANES, 4 * LANES), lambda b, c, l: (l[0], c, 0, 0)),
            pl.BlockSpec((None, None, 1, 4 * LANES), lambda b, c, l: (l[0], c, 0, 0)),
            pl.BlockSpec((None, None, 2, LANES), lambda b, c, l: (l[0], c, 0, 0)),
        ],
        out_specs=pl.BlockSpec((seq_len, LANES), lambda b, c, l: (b, c)),
        scratch_shapes=[pltpu.VMEM((seq_len, LANES), f32)],
    )
    return pl.pallas_call(
        kern,
        grid_spec=grid_spec,
        out_shape=jax.ShapeDtypeStruct((t, LRU_WIDTH), bf16),
        compiler_params=_cparams(("parallel", "parallel"), VMEM_LIMIT),
        name="lru",
    )(lidx, u, lg, p["lru_w"], p["lru_b"], p["lru_lam"])


def _ssd_kernel(l_ref, x_ref, b_ref, c_ref, dt_ref, alog_ref, dsk_ref, o_ref, yp_scr, *, seq_len):
    q = CHUNK
    nc = seq_len // q
    li = lax.broadcasted_iota(i32, (q, q), 0)
    si = lax.broadcasted_iota(i32, (q, q), 1)
    tril = si <= li
    triu = si >= li
    m_l = lax.broadcasted_iota(i32, (q, LANES), 1) < SSD_HEAD_DIM
    lane8 = lax.broadcasted_iota(i32, (8, q), 1)
    row8 = lax.broadcasted_iota(i32, (8, q), 0)
    a_col = -jnp.exp(alog_ref[...])
    pad = jnp.zeros((q - 8, q), f32)

    def lane_cumsum(v, reverse):
        for d in (1, 2, 4, 8, 16, 32, 64):
            if reverse:
                sh = pltpu.roll(v, q - d, 1)
                m = lane8 < q - d
            else:
                sh = pltpu.roll(v, d, 1)
                m = lane8 >= d
            v = v + jnp.where(m, sh, 0.0)
        return v

    def prep(c):
        off = pl.multiple_of(c * q, q)
        x = x_ref[pl.ds(off, q), :].astype(f32)
        bc = b_ref[pl.ds(off, q), :]
        cc = c_ref[pl.ds(off, q), :]
        dt = dt_ref[:, pl.ds(off, q)]
        dta = dt * a_col
        cs = jnp.where(row8 < 2, lane_cumsum(dta, False), lane_cumsum(dta, True))
        cs_t = jnp.concatenate([cs, pad], axis=0).T
        dt_t = jnp.concatenate([dt, pad], axis=0).T
        bt = bc.astype(f32).T.astype(bf16)
        return off, x, bc, cc, bt, cs, cs_t, dt_t

    def fwd(c, s_f):
        off, x, bc, cc, bt, cs, cs_t, dt_t = prep(c)
        cb = lax.dot_general(cc, bc, (((1,), (1,)), ((), ())), preferred_element_type=f32)
        xf = x * jnp.where(m_l, dt_t[:, 0:1], dt_t[:, 1:2])
        xb = x * jnp.where(m_l, dt_t[:, 2:3], dt_t[:, 3:4])
        y = jnp.zeros((q, LANES), f32)
        for k in range(4):
            arg = cs_t[:, k:k + 1] - cs[k:k + 1, :]
            dec = jnp.exp(jnp.where(tril if k < 2 else triu, arg, NEG))
            m = (cb * dec).astype(bf16)
            src = xf if k < 2 else xb
            src = jnp.where(m_l, src, 0.0) if k % 2 == 0 else jnp.where(m_l, 0.0, src)
            y = y + jnp.dot(m, src.astype(bf16), preferred_element_type=f32)
        acol = jnp.where(m_l, cs_t[:, 0:1], cs_t[:, 1:2])
        y = y + jnp.dot(cc, s_f.astype(bf16), preferred_element_type=f32) * jnp.exp(acol)
        last = acol[q - 1:q, :]
        s_f = s_f * jnp.exp(last) + jnp.dot(bt, (xf * jnp.exp(last - acol)).astype(bf16),
                                            preferred_element_type=f32)
        yp_scr[pl.ds(off, q), :] = y
        return s_f

    lax.fori_loop(0, nc, fwd, jnp.zeros((SSD_STATE, LANES), f32))

    def bwd(cr, s_b):
        off, x, bc, cc, bt, cs, cs_t, dt_t = prep(nc - 1 - cr)
        xb = x * jnp.where(m_l, dt_t[:, 2:3], dt_t[:, 3:4])
        acol = jnp.where(m_l, cs_t[:, 2:3], cs_t[:, 3:4])
        y = jnp.dot(cc, s_b.astype(bf16), preferred_element_type=f32) * jnp.exp(acol)
        first = acol[0:1, :]
        s_b = s_b * jnp.exp(first) + jnp.dot(bt, (xb * jnp.exp(first - acol)).astype(bf16),
                                             preferred_element_type=f32)
        o_ref[pl.ds(off, q), :] = (yp_scr[pl.ds(off, q), :] + y + dsk_ref[...] * x).astype(bf16)
        return s_b

    lax.fori_loop(0, nc, bwd, jnp.zeros((SSD_STATE, LANES), f32))


def _ssd(lidx, xbc, dt, p, seq_len):
    t = xbc.shape[0]
    nb = t // seq_len
    kern = functools.partial(_ssd_kernel, seq_len=seq_len)
    nxb = SSD_INNER // LANES
    pairs_per_group = N_PAIR // SSD_GROUPS
    grid_spec = pltpu.PrefetchScalarGridSpec(
        num_scalar_prefetch=1,
        grid=(nb, N_PAIR),
        in_specs=[
            pl.BlockSpec((seq_len, LANES), lambda b, h, l: (b, h)),
            pl.BlockSpec((seq_len, LANES), lambda b, h, l: (b, nxb + h // pairs_per_group)),
            pl.BlockSpec((seq_len, LANES), lambda b, h, l: (b, nxb + SSD_GROUPS + h // pairs_per_group)),
            pl.BlockSpec((8, seq_len), lambda b, h, l: (h, b)),
            pl.BlockSpec((None, None, 8, 1), lambda b, h, l: (l[0], h, 0, 0)),
            pl.BlockSpec((None, None, 1, LANES), lambda b, h, l: (l[0], h, 0, 0)),
        ],
        out_specs=pl.BlockSpec((seq_len, LANES), lambda b, h, l: (b, h)),
        scratch_shapes=[pltpu.VMEM((seq_len, LANES), f32)],
    )
    return pl.pallas_call(
        kern,
        grid_spec=grid_spec,
        out_shape=jax.ShapeDtypeStruct((t, SSD_INNER), bf16),
        compiler_params=_cparams(("parallel", "parallel"), VMEM_LIMIT),
        name="ssd",
    )(lidx, xbc, xbc, xbc, dt, p["ssd_alog"], p["ssd_dskip"])


def _post_kernel(l_ref, ya_ref, ys_ref, z_ref, gr_ref, x_ref, wa_ref, wb_ref, wo_ref, gs_ref, gf_ref,
                 rh_ref, rl_ref, xo_ref, h_ref, lg_ref):
    z = z_ref[...].astype(f32)
    ysg = ys_ref[...].astype(f32) * (z * jax.nn.sigmoid(z))
    yb = ysg * lax.rsqrt(jnp.mean(ysg * ysg, axis=-1, keepdims=True) + NORM_EPS) * gs_ref[...]
    p_a = jnp.dot(ya_ref[...], wa_ref[...], preferred_element_type=f32)
    p_b = jnp.dot(yb.astype(bf16), wb_ref[...], preferred_element_type=f32)
    gates = jax.nn.sigmoid(gr_ref[...].astype(f32))
    merged = gates[:, :D_MODEL] * p_a + gates[:, D_MODEL:] * p_b
    xo = x_ref[...] + jnp.dot(merged.astype(bf16), wo_ref[...], preferred_element_type=f32)
    xo_ref[...] = xo
    h = xo * lax.rsqrt(jnp.mean(xo * xo, axis=-1, keepdims=True) + NORM_EPS) * gf_ref[...]
    h_hi = h.astype(bf16)
    h_lo = (h - h_hi.astype(f32)).astype(bf16)
    h_ref[...] = h_hi
    nt = (((1,), (1,)), ((), ()))
    lg = lax.dot_general(rh_ref[...], h_hi, nt, preferred_element_type=f32)
    lg = lg + lax.dot_general(rh_ref[...], h_lo, nt, preferred_element_type=f32)
    lg = lg + lax.dot_general(rl_ref[...], h_hi, nt, preferred_element_type=f32)
    lg_ref[...] = lg


def _post(lidx, ya, ys, z, gr, x, p):
    t = x.shape[0]

    def wspec(shape):
        return pl.BlockSpec((None,) + shape, lambda i, l: (l[0],) + (0,) * len(shape))

    def rows(width):
        return pl.BlockSpec((TM5, width), lambda i, l: (i, 0))

    grid_spec = pltpu.PrefetchScalarGridSpec(
        num_scalar_prefetch=1,
        grid=(t // TM5,),
        in_specs=[rows(LRU_WIDTH), rows(SSD_INNER), rows(SSD_INNER), rows(2 * D_MODEL), rows(D_MODEL),
                  wspec((LRU_WIDTH, D_MODEL)), wspec((SSD_INNER, D_MODEL)), wspec((D_MODEL, D_MODEL)),
                  wspec((1, SSD_INNER)), wspec((1, D_MODEL)),
                  wspec((N_EXPERTS, D_MODEL)), wspec((N_EXPERTS, D_MODEL))],
        out_specs=[rows(D_MODEL), rows(D_MODEL), pl.BlockSpec((N_EXPERTS, TM5), lambda i, l: (0, i))],
    )
    return pl.pallas_call(
        _post_kernel,
        grid_spec=grid_spec,
        out_shape=[jax.ShapeDtypeStruct((t, D_MODEL), f32),
                   jax.ShapeDtypeStruct((t, D_MODEL), bf16),
                   jax.ShapeDtypeStruct((N_EXPERTS, t), f32)],
        compiler_params=_cparams(("parallel",), 40 * 1024 * 1024),
        name="post",
    )(lidx, ya, ys, z, gr, x, p["w_br_a"], p["w_br_b"], p["w_out"], p["ssd_norm"], p["norm_ffn"],
      p["w_rt_hi"], p["w_rt_lo"])


def _route_kernel(lg_ref, aff_ref, pos_ref, excl_ref, bits_scr, *, cap):
    n = lg_ref.shape[1]
    lg = lg_ref[...]
    ex = jnp.exp(lg - jnp.max(lg, axis=0, keepdims=True))
    aff = ex / jnp.sum(ex, axis=0, keepdims=True)
    aff_ref[...] = aff
    bits_scr[...] = lax.bitcast_convert_type(aff, i32)

    def body(t, prefix):
        cand = prefix | jnp.left_shift(jnp.int32(1), 30 - t)
        cnt = jnp.sum((bits_scr[...] >= cand).astype(f32), axis=1, keepdims=True)
        return jnp.where(cnt >= cap, cand, prefix)

    thr = lax.fori_loop(0, 31, body, jnp.zeros((N_EXPERTS, 1), i32))
    bits = bits_scr[...]
    gt = (bits > thr).astype(f32)
    eq = (bits == thr).astype(f32)
    need = cap - jnp.sum(gt, axis=1, keepdims=True)

    ri = lax.broadcasted_iota(i32, (PFX, PFX), 0)
    ci = lax.broadcasted_iota(i32, (PFX, PFX), 1)
    upper = (ri <= ci).astype(bf16)

    def prefix_count(mask):
        outs = []
        carry = jnp.zeros((N_EXPERTS, 1), f32)
        for k in range(n // PFX):
            blk = mask[:, k * PFX:(k + 1) * PFX].astype(bf16)
            inc = jnp.dot(blk, upper, preferred_element_type=f32) + carry
            carry = inc[:, PFX - 1:PFX]
            outs.append(inc)
        return jnp.concatenate(outs, axis=1)

    sel = gt + eq * (prefix_count(eq) <= need).astype(f32)
    excl = prefix_count(sel) - sel
    excl_ref[...] = excl.astype(i32)
    pos_ref[...] = jnp.where(sel > 0.5, excl, -1.0).astype(i32)


def _route(logits_t, cap):
    n = logits_t.shape[1]
    kern = functools.partial(_route_kernel, cap=cap)
    full = pl.BlockSpec((N_EXPERTS, n), lambda i: (0, 0))
    return pl.pallas_call(
        kern,
        grid=(1,),
        in_specs=[full],
        out_specs=[full, full, full],
        out_shape=[jax.ShapeDtypeStruct((N_EXPERTS, n), f32),
                   jax.ShapeDtypeStruct((N_EXPERTS, n), i32),
                   jax.ShapeDtypeStruct((N_EXPERTS, n), i32)],
        scratch_shapes=[pltpu.VMEM((N_EXPERTS, n), i32)],
        compiler_params=_cparams(("arbitrary",), VMEM_LIMIT),
        name="route",
    )(logits_t)


def _slab_base(c0, cap):
    return jnp.minimum((c0 // BF16_ROWS) * BF16_ROWS, cap - SLAB)


def _dispatch_kernel(tbl_ref, pos_ref, h_ref, xe_ref, *, ntile, cap):
    e = pl.program_id(0)
    k = pl.program_id(1)

    @pl.when(k == 0)
    def _():
        xe_ref[...] = jnp.zeros_like(xe_ref)

    base = pl.multiple_of(_slab_base(tbl_ref[e * ntile + k], cap), BF16_ROWS)
    rel = pos_ref[pl.ds(e, 1), :] - base
    ji = lax.broadcasted_iota(i32, (SLAB, TOK_TILE), 0)
    onehot = (ji == rel).astype(bf16)
    slab = jnp.dot(onehot, h_ref[...], preferred_element_type=f32)
    cur = xe_ref[pl.ds(base, SLAB), :].astype(f32)
    xe_ref[pl.ds(base, SLAB), :] = (cur + slab).astype(bf16)


def _dispatch(tbl, pos, h, cap):
    n = h.shape[0]
    ntile = n // TOK_TILE
    kern = functools.partial(_dispatch_kernel, ntile=ntile, cap=cap)
    grid_spec = pltpu.PrefetchScalarGridSpec(
        num_scalar_prefetch=1,
        grid=(N_EXPERTS, ntile),
        in_specs=[pl.BlockSpec((N_EXPERTS, TOK_TILE), lambda e, k, t: (0, k)),
                  pl.BlockSpec((TOK_TILE, D_MODEL), lambda e, k, t: (k, 0))],
        out_specs=pl.BlockSpec((None, cap, D_MODEL), lambda e, k, t: (e, 0, 0)),
    )
    return pl.pallas_call(
        kern,
        grid_spec=grid_spec,
        out_shape=jax.ShapeDtypeStruct((N_EXPERTS, cap, D_MODEL), bf16),
        compiler_params=_cparams(("parallel", "arbitrary"), 40 * 1024 * 1024),
        name="dispatch",
    )(tbl, pos, h)


def _ffn_kernel(l_ref, x_ref, wg_ref, wu_ref, wd_ref, o_ref, acc_scr):
    f = pl.program_id(2)

    @pl.when(f == 0)
    def _():
        acc_scr[...] = jnp.zeros_like(acc_scr)

    x = x_ref[...]
    g = jnp.dot(x, wg_ref[...], preferred_element_type=f32)
    u = jnp.dot(x, wu_ref[...], preferred_element_type=f32)
    hid = (g * jax.nn.sigmoid(g) * u).astype(bf16)
    acc_scr[...] += jnp.dot(hid, wd_ref[...], preferred_element_type=f32)

    @pl.when(f == pl.num_programs(2) - 1)
    def _():
        o_ref[...] = acc_scr[...].astype(bf16)


def _ffn(lidx, xe, p, cap):
    tmf = min(TMF, cap)
    grid_spec = pltpu.PrefetchScalarGridSpec(
        num_scalar_prefetch=1,
        grid=(N_EXPERTS, cap // tmf, EXPERT_FF // TFF),
        in_specs=[pl.BlockSpec((None, tmf, D_MODEL), lambda e, r, f, l: (e, r, 0)),
                  pl.BlockSpec((None, None, D_MODEL, TFF), lambda e, r, f, l: (l[0], e, 0, f)),
                  pl.BlockSpec((None, None, D_MODEL, TFF), lambda e, r, f, l: (l[0], e, 0, f)),
                  pl.BlockSpec((None, None, TFF, D_MODEL), lambda e, r, f, l: (l[0], e, f, 0))],
        out_specs=pl.BlockSpec((None, tmf, D_MODEL), lambda e, r, f, l: (e, r, 0)),
        scratch_shapes=[pltpu.VMEM((tmf, D_MODEL), f32)],
    )
    return pl.pallas_call(
        _ffn_kernel,
        grid_spec=grid_spec,
        out_shape=jax.ShapeDtypeStruct((N_EXPERTS, cap, D_MODEL), bf16),
        compiler_params=_cparams(("parallel", "parallel", "arbitrary"), 48 * 1024 * 1024),
        name="ffn",
    )(lidx, xe, p["w_gate"], p["w_up"], p["w_down"])


def _combine_kernel(tbl_ref, x_ref, pos_ref, aff_ref, *rest, ntile, cap):
    slabs = rest[:N_EXPERTS]
    o_ref = rest[N_EXPERTS]
    k = pl.program_id(0)
    acc = x_ref[...]
    pos = pos_ref[...]
    aff = aff_ref[...]
    ji = lax.broadcasted_iota(i32, (TOK_TILE, SLAB), 1)
    for e in range(N_EXPERTS):
        base = _slab_base(tbl_ref[e * ntile + k], cap)
        onehot = (ji == pos[:, e:e + 1] - base).astype(bf16)
        acc = acc + aff[:, e:e + 1] * jnp.dot(onehot, slabs[e][0], preferred_element_type=f32)
    o_ref[...] = acc


def _slab_spec(e, ntile, cap):
    def imap(k, t):
        return (e, pl.multiple_of(_slab_base(t[e * ntile + k], cap), BF16_ROWS), 0)
    return pl.BlockSpec((pl.Element(1), pl.Element(SLAB), pl.Element(D_MODEL)), imap)


def _combine(tbl, x, pos_t, aff_t, ye):
    n = x.shape[0]
    ntile = n // TOK_TILE
    cap = ye.shape[1]
    kern = functools.partial(_combine_kernel, ntile=ntile, cap=cap)
    grid_spec = pltpu.PrefetchScalarGridSpec(
        num_scalar_prefetch=1,
        grid=(ntile,),
        in_specs=[pl.BlockSpec((TOK_TILE, D_MODEL), lambda k, t: (k, 0)),
                  pl.BlockSpec((TOK_TILE, N_EXPERTS), lambda k, t: (k, 0)),
                  pl.BlockSpec((TOK_TILE, N_EXPERTS), lambda k, t: (k, 0))]
        + [_slab_spec(e, ntile, cap) for e in range(N_EXPERTS)],
        out_specs=pl.BlockSpec((TOK_TILE, D_MODEL), lambda k, t: (k, 0)),
    )
    return pl.pallas_call(
        kern,
        grid_spec=grid_spec,
        out_shape=jax.ShapeDtypeStruct((n, D_MODEL), f32),
        compiler_params=_cparams(("parallel",), 40 * 1024 * 1024),
        name="combine",
    )(tbl, x, pos_t, aff_t, *([ye] * N_EXPERTS))


def _norm_kernel(x_ref, g_ref, o_ref):
    x = x_ref[...]
    o_ref[...] = x * lax.rsqrt(jnp.mean(x * x, axis=-1, keepdims=True) + NORM_EPS) * g_ref[...]


def _final_norm(x, g):
    t = x.shape[0]
    return pl.pallas_call(
        _norm_kernel,
        grid=(t // TM1,),
        in_specs=[pl.BlockSpec((TM1, D_MODEL), lambda i: (i, 0)), pl.BlockSpec((1, D_MODEL), lambda i: (0, 0))],
        out_specs=pl.BlockSpec((TM1, D_MODEL), lambda i: (i, 0)),
        out_shape=jax.ShapeDtypeStruct((t, D_MODEL), f32),
        compiler_params=_cparams(("parallel",)),
        name="final_norm",
    )(x, g)


def _block_diag_pairs(w):
    d = w.shape[0]
    w = w.reshape(d, LRU_BLOCKS // 2, 2, LRU_BLOCK, LRU_BLOCK)
    zero = jnp.zeros_like(w[:, :, 0])
    top = jnp.concatenate([w[:, :, 0], zero], axis=-1)
    bot = jnp.concatenate([zero, w[:, :, 1]], axis=-1)
    return jnp.concatenate([top, bot], axis=-2)


def _pair_rows(v):
    d = v.shape[0]
    rest = v.shape[3:]
    v = jnp.moveaxis(v.reshape((d, 2, N_PAIR, 2) + rest), 1, 2).reshape((d, N_PAIR, 4) + rest)
    return jnp.concatenate([v, jnp.zeros_like(v)], axis=2).reshape((d, DT_ROWS) + rest)


def _prepare(norm_mix, w_in, lru_conv_w, lru_conv_b, lru_wa, lru_ba, lru_wx, lru_bx, lru_lambda,
             ssd_conv_w, ssd_conv_b, ssd_dt_bias, ssd_a_log, ssd_d, ssd_norm, w_branch, w_out,
             norm_ffn, w_router, w_gate, w_up, w_down):
    c_dt = 2 * LRU_WIDTH + SSD_INNER + SSD_XBC
    n_dt = 2 * SSD_HEADS
    w_main = jnp.concatenate([w_in[:, :, :c_dt], w_in[:, :, c_dt + n_dt:]], axis=-1).astype(bf16)
    w_dt = jnp.swapaxes(w_in[:, :, c_dt:c_dt + n_dt], 1, 2).reshape(DEPTH, 2, SSD_HEADS, D_MODEL)
    w_dt = _pair_rows(w_dt).astype(bf16)

    ident = jnp.zeros((DEPTH, CONV_K, 1), f32)
    conv_w = jnp.concatenate([
        lru_conv_w, jnp.broadcast_to(ident, (DEPTH, CONV_K, LRU_WIDTH + SSD_INNER)),
        ssd_conv_w, jnp.broadcast_to(ident, (DEPTH, CONV_K, 2 * D_MODEL))], axis=-1)
    conv_b = jnp.concatenate([
        lru_conv_b, jnp.zeros((DEPTH, LRU_WIDTH + SSD_INNER), f32),
        ssd_conv_b, jnp.zeros((DEPTH, 2 * D_MODEL), f32)], axis=-1)[:, None, :]

    lru_w = jnp.concatenate([_block_diag_pairs(lru_wa[:, 0]), _block_diag_pairs(lru_wx[:, 0]),
                             _block_diag_pairs(lru_wa[:, 1]), _block_diag_pairs(lru_wx[:, 1])], axis=-1).astype(bf16)
    ncb = LRU_WIDTH // LANES

    def chan(v):
        return v.reshape(DEPTH, ncb, 1, LANES)

    lru_b = jnp.concatenate([chan(lru_ba[:, 0]), chan(lru_bx[:, 0]), chan(lru_ba[:, 1]), chan(lru_bx[:, 1])], axis=-1)
    lru_lam = lru_lambda.reshape(DEPTH, 2, ncb, LANES).transpose(0, 2, 1, 3)

    return dict(
        norm_mix=norm_mix[:, None, :],
        w_main=w_main, w_dt=w_dt, b_dt=_pair_rows(ssd_dt_bias)[:, :, None],
        conv_w=conv_w, conv_b=conv_b,
        lru_w=lru_w, lru_b=lru_b, lru_lam=lru_lam,
        ssd_alog=_pair_rows(ssd_a_log).reshape(DEPTH, N_PAIR, 8, 1),
        ssd_dskip=jnp.repeat(ssd_d, SSD_HEAD_DIM, axis=-1).reshape(DEPTH, N_PAIR, 1, LANES),
        w_br_a=w_branch[:, :LRU_WIDTH].astype(bf16), w_br_b=w_branch[:, LRU_WIDTH:].astype(bf16),
        w_out=w_out.astype(bf16),
        ssd_norm=ssd_norm[:, None, :], norm_ffn=norm_ffn[:, None, :],
        w_rt_hi=_split_hi(jnp.swapaxes(w_router, 1, 2)), w_rt_lo=_split_lo(jnp.swapaxes(w_router, 1, 2)),
        w_gate=w_gate.astype(bf16), w_up=w_up.astype(bf16), w_down=w_down.astype(bf16),
    )


def _split_hi(w):
    return w.astype(bf16)


def _split_lo(w):
    return (w - w.astype(bf16).astype(f32)).astype(bf16)


def _layer(lidx, x, p, seq_len):
    n = x.shape[0]
    cap = CAPACITY_FACTOR * n // N_EXPERTS
    u, lg, z, xbc, gr, dt = _inproj(lidx, x, p, seq_len)
    ya = _lru(lidx, u, lg, p, seq_len)
    ys = _ssd(lidx, xbc, dt, p, seq_len)
    xo, h, logits_t = _post(lidx, ya, ys, z, gr, x, p)
    aff, pos, excl = _route(logits_t, cap)
    tbl = excl[:, ::TOK_TILE].reshape(-1)
    xe = _dispatch(tbl, pos, h, cap)
    ye = _ffn(lidx, xe, p, cap)
    return _combine(tbl, xo, pos.T, aff.T, ye)


def kernel(x_prompt, x_sample, norm_mix, w_in, lru_conv_w, lru_conv_b, lru_wa, lru_ba, lru_wx, lru_bx, lru_lambda, ssd_conv_w, ssd_conv_b, ssd_dt_bias, ssd_a_log, ssd_d, ssd_norm, w_branch, w_out, norm_ffn, w_router, w_gate, w_up, w_down, norm_final):
    p = _prepare(norm_mix, w_in, lru_conv_w, lru_conv_b, lru_wa, lru_ba, lru_wx, lru_bx, lru_lambda,
                 ssd_conv_w, ssd_conv_b, ssd_dt_bias, ssd_a_log, ssd_d, ssd_norm, w_branch, w_out,
                 norm_ffn, w_router, w_gate, w_up, w_down)
    bp, lp, _ = x_prompt.shape
    bs, ls, _ = x_sample.shape

    def body(i, xs):
        lidx = jnp.reshape(i, (1,)).astype(i32)
        return (_layer(lidx, xs[0], p, lp), _layer(lidx, xs[1], p, ls))

    xp, xs = lax.fori_loop(0, DEPTH, body, (x_prompt.reshape(bp * lp, D_MODEL), x_sample.reshape(bs * ls, D_MODEL)))
    g = norm_final[None, :]
    return (_final_norm(xp, g).reshape(bp, lp, D_MODEL), _final_norm(xs, g).reshape(bs, ls, D_MODEL))
```

```python
import functools

import jax
import jax.numpy as jnp
from jax import lax
from jax.experimental import pallas as pl
from jax.experimental.pallas import tpu as pltpu

f32 = jnp.float32
bf16 = jnp.bfloat16
i32 = jnp.int32

D_MODEL = 1024
DEPTH = 4
LRU_WIDTH = 1024
LRU_BLOCKS = 16
LRU_BLOCK = LRU_WIDTH // LRU_BLOCKS
LRU_C = 8.0
CONV_K = 4
SSD_INNER = 2 * D_MODEL
SSD_HEAD_DIM = 64
SSD_HEADS = SSD_INNER // SSD_HEAD_DIM
SSD_GROUPS = 4
SSD_STATE = 128
SSD_XBC = SSD_INNER + 2 * SSD_GROUPS * SSD_STATE
N_EXPERTS = 16
EXPERT_FF = 2048
CAPACITY_FACTOR = 2
NORM_EPS = 1e-6

LANES = 128
BF16_ROWS = 16
VMEM_LIMIT = 56 * 1024 * 1024

TM1 = 1024
TN1 = 512
HALO = BF16_ROWS
N_MAIN = 2 * LRU_WIDTH + SSD_INNER + SSD_XBC + 2 * D_MODEL
NJ1 = N_MAIN // TN1
N_PAIR = SSD_HEADS // 2
DT_ROWS = 8 * N_PAIR

SCAN_ROWS = 128
CHUNK = 128
SSD_GROUP = 8
TOK_TILE = 128
DISP_TILES = 8
SLAB = TOK_TILE + BF16_ROWS
TM5 = 512
SUB5 = 256
TMF = 1024
TFF = 512
PFX = 512
NEG = -1e30


def _cparams(sem, vmem=None):
    return pltpu.CompilerParams(dimension_semantics=sem, vmem_limit_bytes=vmem)


def _inproj_kernel(l_ref, x_ref, xp_ref, xn_ref, g_ref, w_ref, cw_ref, cb_ref, wdt_ref, bdt_ref,
                   u_ref, lg_ref, z_ref, xbc_ref, gr_ref, dt_ref, h_scr, a_scr, *, tiles_per_seq):
    i = pl.program_id(0)
    j = pl.program_id(1)

    @pl.when(j == 0)
    def _():
        g = g_ref[...]

        def nrm(x):
            return x * lax.rsqrt(jnp.mean(x * x, axis=-1, keepdims=True) + NORM_EPS) * g

        it = i % tiles_per_seq
        top_ok = (it != 0).astype(f32)
        bot_ok = (it != tiles_per_seq - 1).astype(f32)
        h_scr[0:HALO, :] = (nrm(xp_ref[...]) * top_ok).astype(bf16)
        hm = nrm(x_ref[...]).astype(bf16)
        h_scr[HALO:HALO + TM1, :] = hm
        h_scr[HALO + TM1:, :] = (nrm(xn_ref[...]) * bot_ok).astype(bf16)
        raw = lax.dot_general(wdt_ref[...], hm, (((1,), (1,)), ((), ())), preferred_element_type=f32)
        dt_ref[...] = jax.nn.softplus(raw + bdt_ref[...])

    def plain():
        return jnp.dot(h_scr[HALO:HALO + TM1, :], w_ref[...], preferred_element_type=f32).astype(bf16)

    def conv():
        a_scr[...] = jnp.dot(h_scr[...], w_ref[...], preferred_element_type=f32)
        cw = cw_ref[...]
        out = cb_ref[...] + cw[0:1] * a_scr[HALO - 2:HALO - 2 + TM1, :]
        out = out + cw[1:2] * a_scr[HALO - 1:HALO - 1 + TM1, :]
        out = out + cw[2:3] * a_scr[HALO:HALO + TM1, :]
        out = out + cw[3:4] * a_scr[HALO + 1:HALO + 1 + TM1, :]
        return out

    @pl.when(j < 2)
    def _():
        u_ref[...] = conv().astype(bf16)

    @pl.when((j >= 2) & (j < 4))
    def _():
        lg_ref[...] = plain()

    @pl.when((j >= 4) & (j < 8))
    def _():
        z_ref[...] = plain()

    @pl.when((j >= 8) & (j < 14))
    def _():
        v = conv()
        xbc_ref[...] = (v * jax.nn.sigmoid(v)).astype(bf16)

    @pl.when(j >= 14)
    def _():
        gr_ref[...] = plain()


def _inproj(lidx, x, p, seq_len):
    t = x.shape[0]
    nrow = t // TM1
    hb = TM1 // HALO
    nhb = t // HALO
    kern = functools.partial(_inproj_kernel, tiles_per_seq=seq_len // TM1)

    def seg(lo, hi):
        return lambda i, j, l: (i, jnp.clip(j, lo, hi) - lo)

    grid_spec = pltpu.PrefetchScalarGridSpec(
        num_scalar_prefetch=1,
        grid=(nrow, NJ1),
        in_specs=[
            pl.BlockSpec((TM1, D_MODEL), lambda i, j, l: (i, 0)),
            pl.BlockSpec((HALO, D_MODEL), lambda i, j, l: (jnp.maximum(i * hb - 1, 0), 0)),
            pl.BlockSpec((HALO, D_MODEL), lambda i, j, l: (jnp.minimum((i + 1) * hb, nhb - 1), 0)),
            pl.BlockSpec((None, 1, D_MODEL), lambda i, j, l: (l[0], 0, 0)),
            pl.BlockSpec((None, D_MODEL, TN1), lambda i, j, l: (l[0], 0, j)),
            pl.BlockSpec((None, CONV_K, TN1), lambda i, j, l: (l[0], 0, j)),
            pl.BlockSpec((None, 1, TN1), lambda i, j, l: (l[0], 0, j)),
            pl.BlockSpec((None, DT_ROWS, D_MODEL), lambda i, j, l: (l[0], 0, 0)),
            pl.BlockSpec((None, DT_ROWS, 1), lambda i, j, l: (l[0], 0, 0)),
        ],
        out_specs=[
            pl.BlockSpec((TM1, TN1), seg(0, 1)),
            pl.BlockSpec((TM1, TN1), seg(2, 3)),
            pl.BlockSpec((TM1, TN1), seg(4, 7)),
            pl.BlockSpec((TM1, TN1), seg(8, 13)),
            pl.BlockSpec((TM1, TN1), seg(14, 17)),
            pl.BlockSpec((DT_ROWS, TM1), lambda i, j, l: (0, i)),
        ],
        scratch_shapes=[pltpu.VMEM((TM1 + 2 * HALO, D_MODEL), bf16),
                        pltpu.VMEM((TM1 + 2 * HALO, TN1), f32)],
    )
    return pl.pallas_call(
        kern,
        grid_spec=grid_spec,
        out_shape=[
            jax.ShapeDtypeStruct((t, LRU_WIDTH), bf16),
            jax.ShapeDtypeStruct((t, LRU_WIDTH), bf16),
            jax.ShapeDtypeStruct((t, SSD_INNER), bf16),
            jax.ShapeDtypeStruct((t, SSD_XBC), bf16),
            jax.ShapeDtypeStruct((t, 2 * D_MODEL), bf16),
            jax.ShapeDtypeStruct((DT_ROWS, t), f32),
        ],
        compiler_params=_cparams(("parallel", "arbitrary"), 40 * 1024 * 1024),
        name="inproj",
    )(lidx, x, x, x, p["norm_mix"], p["w_main"], p["conv_w"], p["conv_b"], p["w_dt"], p["b_dt"])


def _lru_kernel(l_ref, u_ref, g_ref, w_ref, b_ref, lam_ref, o_ref, hf_scr, *, seq_len):
    r = SCAN_ROWS
    nsub = seq_len // r
    ngrp = r // 8
    c8 = LRU_C * jax.nn.log_sigmoid(lam_ref[...])
    row = lax.broadcasted_iota(i32, (r, LANES), 0) & 7

    def gates(j, d):
        off = pl.multiple_of(j * r, r)
        u = u_ref[pl.ds(off, r), :]
        g = jnp.dot(u, w_ref[:, d * 2 * LANES:(d + 1) * 2 * LANES], preferred_element_type=f32)
        g = g + b_ref[:, d * 2 * LANES:(d + 1) * 2 * LANES]
        rg = jax.nn.sigmoid(g[:, :LANES])
        ig = jax.nn.sigmoid(g[:, LANES:])
        a = jnp.exp(c8[d:d + 1] * rg)
        v = 1.0 - a * a
        b = jnp.where(v > 0.0, v * lax.rsqrt(v), 0.0) * (ig * u.astype(f32))
        return off, a, b

    def scan8(a, b, reverse):
        for d in (1, 2, 4):
            if reverse:
                m = row <= 7 - d
                sh = r - d
            else:
                m = row >= d
                sh = d
            a_s = jnp.where(m, pltpu.roll(a, sh, 0), 1.0)
            b_s = jnp.where(m, pltpu.roll(b, sh, 0), 0.0)
            b = b + a * b_s
            a = a * a_s
        return a, b

    def carry(a, b, c, reverse):
        hs = [None] * ngrp
        order = range(ngrp - 1, -1, -1) if reverse else range(ngrp)
        for k in order:
            h = b[8 * k:8 * k + 8] + a[8 * k:8 * k + 8] * c
            c = h[0:1] if reverse else h[7:8]
            hs[k] = h
        return jnp.concatenate(hs, axis=0), c

    def fwd(j, c):
        off, a, b = gates(j, 0)
        a, b = scan8(a, b, False)
        h, c = carry(a, b, c, False)
        hf_scr[pl.ds(off, r), :] = h
        return c

    lax.fori_loop(0, nsub, fwd, jnp.zeros((1, LANES), f32), unroll=4)

    def bwd(jj, c):
        off, a, b = gates(nsub - 1 - jj, 1)
        a, b = scan8(a, b, True)
        h, c = carry(a, b, c, True)
        gate = jax.nn.gelu(g_ref[pl.ds(off, r), :].astype(f32))
        o_ref[pl.ds(off, r), :] = ((hf_scr[pl.ds(off, r), :] + h) * gate).astype(bf16)
        return c

    lax.fori_loop(0, nsub, bwd, jnp.zeros((1, LANES), f32), unroll=4)


def _lru(lidx, u, lg, p, seq_len):
    t = u.shape[0]
    nb = t // seq_len
    ncb = LRU_WIDTH // LANES
    kern = functools.partial(_lru_kernel, seq_len=seq_len)
    grid_spec = pltpu.PrefetchScalarGridSpec(
        num_scalar_prefetch=1,
        grid=(nb, ncb),
        in_specs=[
            pl.BlockSpec((seq_len, LANES), lambda b, c, l: (b, c)),
            pl.BlockSpec((seq_len, LANES), lambda b, c, l: (b, c)),
            pl.BlockSpec((None, None, LANES, 4 * LANES), lambda b, c, l: (l[0], c, 0, 0)),
            pl.BlockSpec((None, None, 1, 4 * LANES), lambda b, c, l: (l[0], c, 0, 0)),
            pl.BlockSpec((None, None, 2, LANES), lambda b, c, l: (l[0], c, 0, 0)),
        ],
        out_specs=pl.BlockSpec((seq_len, LANES), lambda b, c, l: (b, c)),
        scratch_shapes=[pltpu.VMEM((seq_len, LANES), f32)],
    )
    return pl.pallas_call(
        kern,
        grid_spec=grid_spec,
        out_shape=jax.ShapeDtypeStruct((t, LRU_WIDTH), bf16),
        compiler_params=_cparams(("parallel", "parallel"), VMEM_LIMIT),
        name="lru",
    )(lidx, u, lg, p["lru_w"], p["lru_b"], p["lru_lam"])


def _ssd_kernel(l_ref, x_ref, b_ref, c_ref, dt_ref, alog_ref, dsk_ref, o_ref,
                yp_scr, ub_scr, cs_scr, tc_scr, ef_scr, uf_scr, *, seq_len):
    q = CHUNK
    grp = SSD_GROUP
    gl = grp * q
    ngrp = seq_len // gl
    li = lax.broadcasted_iota(i32, (q, q), 0)
    si = lax.broadcasted_iota(i32, (q, q), 1)
    tril = si <= li
    triu = si >= li
    eye = (si == li).astype(bf16)
    m_l = lax.broadcasted_iota(i32, (q, LANES), 1) < SSD_HEAD_DIM
    a_col = -jnp.exp(alog_ref[...])

    def tables(g, carry):
        off = pl.multiple_of(g * gl, gl)
        lane = lax.broadcasted_iota(i32, (8, gl), 1) & (q - 1)
        fwd_rows = lax.broadcasted_iota(i32, (8, gl), 0) < 2
        dt = dt_ref[:, pl.ds(off, gl)]
        pre = dt * a_col
        suf = pre
        for d in (1, 2, 4, 8, 16, 32, 64):
            pre = pre + jnp.where(lane >= d, pltpu.roll(pre, d, 1), 0.0)
            suf = suf + jnp.where(lane < q - d, pltpu.roll(suf, gl - d, 1), 0.0)
        cs = jnp.where(fwd_rows, pre, suf)
        cs_scr[:, pl.ds(off, gl)] = cs
        stack = jnp.concatenate([cs[:, c * q:(c + 1) * q] for c in range(grp)]
                                + [jnp.zeros((q - 8 * grp, q), f32)], axis=0)
        tc_scr[g] = stack.T
        return carry

    lax.fori_loop(0, ngrp, tables, 0, unroll=2)

    nt = (((1,), (1,)), ((), ()))

    def fwd(g, s_f):
        tcol = tc_scr[g]
        offs = [pl.multiple_of(g * gl + c * q, q) for c in range(grp)]
        for c in range(grp):
            off = offs[c]
            x = x_ref[pl.ds(off, q), :]
            xl = jnp.where(m_l, x, jnp.zeros_like(x))
            xr = jnp.where(m_l, jnp.zeros_like(x), x)
            bc = b_ref[pl.ds(off, q), :]
            cs = cs_scr[:, pl.ds(off, q)]
            dt = dt_ref[:, pl.ds(off, q)]
            r = lax.dot_general(jnp.concatenate([c_ref[pl.ds(off, q), :], eye], axis=0), bc, nt,
                                preferred_element_type=f32)
            cb = r[:q]
            bt = r[q:]
            colb = [jnp.broadcast_to(tcol[:, 8 * c + k:8 * c + k + 1], (q, q)) for k in range(4)]
            acc = None
            for h in range(2):
                dec = jnp.exp(jnp.where(tril, colb[h] - cs[h:h + 1, :], NEG)) * dt[h:h + 1, :]
                dec = dec + jnp.exp(jnp.where(triu, colb[2 + h] - cs[2 + h:3 + h, :], NEG)) * dt[2 + h:3 + h, :]
                wf = dt[h:h + 1, :] * jnp.exp(cs[h:h + 1, q - 1:q] - cs[h:h + 1, :])
                wb = dt[2 + h:3 + h, :] * jnp.exp(cs[2 + h:3 + h, 0:1] - cs[2 + h:3 + h, :])
                lhs = jnp.concatenate([(cb * dec).astype(bf16), (bt * wf).astype(bf16), (bt * wb).astype(bf16)], axis=0)
                part = jnp.dot(lhs, xl if h == 0 else xr, preferred_element_type=f32)
                acc = part if acc is None else acc + part
            yp_scr[pl.ds(off, q), :] = acc[:q]
            uf_scr[c] = acc[q:2 * q]
            ub_scr[pl.ds(off, q), :] = acc[2 * q:]
            ef_scr[c] = jnp.exp(jnp.where(m_l, colb[0], colb[1]))
        for c in range(grp):
            off = offs[c]
            ef = ef_scr[c]
            y = jnp.dot(c_ref[pl.ds(off, q), :], s_f.astype(bf16), preferred_element_type=f32) * ef
            yp_scr[pl.ds(off, q), :] = yp_scr[pl.ds(off, q), :] + y
            s_f = s_f * ef[q - 1:q, :] + uf_scr[c]
        return s_f

    lax.fori_loop(0, ngrp, fwd, jnp.zeros((SSD_STATE, LANES), f32))

    def bwd(gr, s_b):
        g = ngrp - 1 - gr
        tcol = tc_scr[g]
        for c in range(grp - 1, -1, -1):
            off = pl.multiple_of(g * gl + c * q, q)
            eb = jnp.exp(jnp.where(m_l, jnp.broadcast_to(tcol[:, 8 * c + 2:8 * c + 3], (q, q)),
                                   jnp.broadcast_to(tcol[:, 8 * c + 3:8 * c + 4], (q, q))))
            y = jnp.dot(c_ref[pl.ds(off, q), :], s_b.astype(bf16), preferred_element_type=f32) * eb
            s_b = s_b * eb[0:1, :] + ub_scr[pl.ds(off, q), :]
            x = x_ref[pl.ds(off, q), :].astype(f32)
            o_ref[pl.ds(off, q), :] = (yp_scr[pl.ds(off, q), :] + y + dsk_ref[...] * x).astype(bf16)
        return s_b

    lax.fori_loop(0, ngrp, bwd, jnp.zeros((SSD_STATE, LANES), f32))


def _ssd(lidx, xbc, dt, p, seq_len):
    t = xbc.shape[0]
    nb = t // seq_len
    kern = functools.partial(_ssd_kernel, seq_len=seq_len)
    nxb = SSD_INNER // LANES
    pairs_per_group = N_PAIR // SSD_GROUPS
    grid_spec = pltpu.PrefetchScalarGridSpec(
        num_scalar_prefetch=1,
        grid=(nb, N_PAIR),
        in_specs=[
            pl.BlockSpec((seq_len, LANES), lambda b, h, l: (b, h)),
            pl.BlockSpec((seq_len, LANES), lambda b, h, l: (b, nxb + h // pairs_per_group)),
            pl.BlockSpec((seq_len, LANES), lambda b, h, l: (b, nxb + SSD_GROUPS + h // pairs_per_group)),
            pl.BlockSpec((8, seq_len), lambda b, h, l: (h, b)),
            pl.BlockSpec((None, None, 8, 1), lambda b, h, l: (l[0], h, 0, 0)),
            pl.BlockSpec((None, None, 1, LANES), lambda b, h, l: (l[0], h, 0, 0)),
        ],
        out_specs=pl.BlockSpec((seq_len, LANES), lambda b, h, l: (b, h)),
        scratch_shapes=[pltpu.VMEM((seq_len, LANES), f32),
                        pltpu.VMEM((seq_len, LANES), f32),
                        pltpu.VMEM((8, seq_len), f32),
                        pltpu.VMEM((seq_len // (SSD_GROUP * CHUNK), CHUNK, LANES), f32),
                        pltpu.VMEM((SSD_GROUP, CHUNK, LANES), f32),
                        pltpu.VMEM((SSD_GROUP, SSD_STATE, LANES), f32)],
    )
    return pl.pallas_call(
        kern,
        grid_spec=grid_spec,
        out_shape=jax.ShapeDtypeStruct((t, SSD_INNER), bf16),
        compiler_params=_cparams(("parallel", "parallel"), VMEM_LIMIT),
        name="ssd",
    )(lidx, xbc, xbc, xbc, dt, p["ssd_alog"], p["ssd_dskip"])


def _post_kernel(l_ref, ya_ref, ys_ref, z_ref, gr_ref, x_ref, wa_ref, wb_ref, wo_ref, gs_ref, gf_ref,
                 rh_ref, rl_ref, xo_ref, h_ref, lg_ref):
    nt = (((1,), (1,)), ((), ()))
    for s in range(TM5 // SUB5):
        rs = slice(s * SUB5, (s + 1) * SUB5)
        z = z_ref[rs, :].astype(f32)
        ysg = ys_ref[rs, :].astype(f32) * (z * jax.nn.sigmoid(z))
        yb = ysg * lax.rsqrt(jnp.mean(ysg * ysg, axis=-1, keepdims=True) + NORM_EPS) * gs_ref[...]
        p_a = jnp.dot(ya_ref[rs, :], wa_ref[...], preferred_element_type=f32)
        p_b = jnp.dot(yb.astype(bf16), wb_ref[...], preferred_element_type=f32)
        gates = jax.nn.sigmoid(gr_ref[rs, :].astype(f32))
        merged = gates[:, :D_MODEL] * p_a + gates[:, D_MODEL:] * p_b
        xo = x_ref[rs, :] + jnp.dot(merged.astype(bf16), wo_ref[...], preferred_element_type=f32)
        xo_ref[rs, :] = xo
        h = xo * lax.rsqrt(jnp.mean(xo * xo, axis=-1, keepdims=True) + NORM_EPS) * gf_ref[...]
        h_hi = h.astype(bf16)
        h_lo = (h - h_hi.astype(f32)).astype(bf16)
        h_ref[rs, :] = h_hi
        lg = lax.dot_general(rh_ref[...], h_hi, nt, preferred_element_type=f32)
        lg = lg + lax.dot_general(rh_ref[...], h_lo, nt, preferred_element_type=f32)
        lg = lg + lax.dot_general(rl_ref[...], h_hi, nt, preferred_element_type=f32)
        lg_ref[:, rs] = lg


def _post(lidx, ya, ys, z, gr, x, p):
    t = x.shape[0]

    def wspec(shape):
        return pl.BlockSpec((None,) + shape, lambda i, l: (l[0],) + (0,) * len(shape))

    def rows(width):
        return pl.BlockSpec((TM5, width), lambda i, l: (i, 0))

    grid_spec = pltpu.PrefetchScalarGridSpec(
        num_scalar_prefetch=1,
        grid=(t // TM5,),
        in_specs=[rows(LRU_WIDTH), rows(SSD_INNER), rows(SSD_INNER), rows(2 * D_MODEL), rows(D_MODEL),
                  wspec((LRU_WIDTH, D_MODEL)), wspec((SSD_INNER, D_MODEL)), wspec((D_MODEL, D_MODEL)),
                  wspec((1, SSD_INNER)), wspec((1, D_MODEL)),
                  wspec((N_EXPERTS, D_MODEL)), wspec((N_EXPERTS, D_MODEL))],
        out_specs=[rows(D_MODEL), rows(D_MODEL), pl.BlockSpec((N_EXPERTS, TM5), lambda i, l: (0, i))],
    )
    return pl.pallas_call(
        _post_kernel,
        grid_spec=grid_spec,
        out_shape=[jax.ShapeDtypeStruct((t, D_MODEL), f32),
                   jax.ShapeDtypeStruct((t, D_MODEL), bf16),
                   jax.ShapeDtypeStruct((N_EXPERTS, t), f32)],
        compiler_params=_cparams(("parallel",), 48 * 1024 * 1024),
        name="post",
    )(lidx, ya, ys, z, gr, x, p["w_br_a"], p["w_br_b"], p["w_out"], p["ssd_norm"], p["norm_ffn"],
      p["w_rt_hi"], p["w_rt_lo"])


def _route_kernel(lg_ref, aff_ref, pos_ref, excl_ref, bits_scr, *, cap):
    n = lg_ref.shape[1]
    lg = lg_ref[...]
    ex = jnp.exp(lg - jnp.max(lg, axis=0, keepdims=True))
    aff = ex / jnp.sum(ex, axis=0, keepdims=True)
    aff_ref[...] = aff
    bits_scr[...] = lax.bitcast_convert_type(aff, i32)

    def body(t, prefix):
        cand = prefix | jnp.left_shift(jnp.int32(1), 30 - t)
        cnt = jnp.sum((bits_scr[...] >= cand).astype(f32), axis=1, keepdims=True)
        return jnp.where(cnt >= cap, cand, prefix)

    thr = lax.fori_loop(0, 31, body, jnp.zeros((N_EXPERTS, 1), i32))
    bits = bits_scr[...]
    gt = (bits > thr).astype(f32)
    eq = (bits == thr).astype(f32)
    need = cap - jnp.sum(gt, axis=1, keepdims=True)

    ri = lax.broadcasted_iota(i32, (PFX, PFX), 0)
    ci = lax.broadcasted_iota(i32, (PFX, PFX), 1)
    upper = (ri <= ci).astype(bf16)

    def prefix_count(mask):
        outs = []
        carry = jnp.zeros((N_EXPERTS, 1), f32)
        for k in range(n // PFX):
            blk = mask[:, k * PFX:(k + 1) * PFX].astype(bf16)
            inc = jnp.dot(blk, upper, preferred_element_type=f32) + carry
            carry = inc[:, PFX - 1:PFX]
            outs.append(inc)
        return jnp.concatenate(outs, axis=1)

    sel = gt + eq * (prefix_count(eq) <= need).astype(f32)
    excl = prefix_count(sel) - sel
    excl_ref[...] = excl.astype(i32)
    pos_ref[...] = jnp.where(sel > 0.5, excl, -1.0).astype(i32)


def _route(logits_t, cap):
    n = logits_t.shape[1]
    kern = functools.partial(_route_kernel, cap=cap)
    full = pl.BlockSpec((N_EXPERTS, n), lambda i: (0, 0))
    return pl.pallas_call(
        kern,
        grid=(1,),
        in_specs=[full],
        out_specs=[full, full, full],
        out_shape=[jax.ShapeDtypeStruct((N_EXPERTS, n), f32),
                   jax.ShapeDtypeStruct((N_EXPERTS, n), i32),
                   jax.ShapeDtypeStruct((N_EXPERTS, n), i32)],
        scratch_shapes=[pltpu.VMEM((N_EXPERTS, n), i32)],
        compiler_params=_cparams(("arbitrary",), VMEM_LIMIT),
        name="route",
    )(logits_t)


def _slab_base(c0, cap):
    return jnp.minimum((c0 // BF16_ROWS) * BF16_ROWS, cap - SLAB)


def _dispatch_kernel(tbl_ref, pos_ref, h_ref, xe_ref, *, ntile, cap):
    e = pl.program_id(0)
    k = pl.program_id(1)

    @pl.when(k == 0)
    def _():
        xe_ref[...] = jnp.zeros_like(xe_ref)

    ji = lax.broadcasted_iota(i32, (SLAB, TOK_TILE), 0)
    pos = pos_ref[pl.ds(e, 1), :]
    for s in range(DISP_TILES):
        base = pl.multiple_of(_slab_base(tbl_ref[e * ntile + k * DISP_TILES + s], cap), BF16_ROWS)
        onehot = (ji == pos[:, s * TOK_TILE:(s + 1) * TOK_TILE] - base).astype(bf16)
        slab = jnp.dot(onehot, h_ref[s * TOK_TILE:(s + 1) * TOK_TILE, :], preferred_element_type=f32)
        xe_ref[pl.ds(base, SLAB), :] = xe_ref[pl.ds(base, SLAB), :] + slab.astype(bf16)


def _dispatch(tbl, pos, h, cap):
    n = h.shape[0]
    ntile = n // TOK_TILE
    kern = functools.partial(_dispatch_kernel, ntile=ntile, cap=cap)
    grid_spec = pltpu.PrefetchScalarGridSpec(
        num_scalar_prefetch=1,
        grid=(N_EXPERTS, ntile // DISP_TILES),
        in_specs=[pl.BlockSpec((N_EXPERTS, DISP_TILES * TOK_TILE), lambda e, k, t: (0, k)),
                  pl.BlockSpec((DISP_TILES * TOK_TILE, D_MODEL), lambda e, k, t: (k, 0))],
        out_specs=pl.BlockSpec((None, cap, D_MODEL), lambda e, k, t: (e, 0, 0)),
    )
    return pl.pallas_call(
        kern,
        grid_spec=grid_spec,
        out_shape=jax.ShapeDtypeStruct((N_EXPERTS, cap, D_MODEL), bf16),
        compiler_params=_cparams(("parallel", "arbitrary"), 40 * 1024 * 1024),
        name="dispatch",
    )(tbl, pos, h)


def _ffn_kernel(l_ref, x_ref, wg_ref, wu_ref, wd_ref, o_ref, acc_scr):
    f = pl.program_id(2)

    @pl.when(f == 0)
    def _():
        acc_scr[...] = jnp.zeros_like(acc_scr)

    x = x_ref[...]
    g = jnp.dot(x, wg_ref[...], preferred_element_type=f32)
    u = jnp.dot(x, wu_ref[...], preferred_element_type=f32)
    hid = (g * jax.nn.sigmoid(g) * u).astype(bf16)
    acc_scr[...] += jnp.dot(hid, wd_ref[...], preferred_element_type=f32)

    @pl.when(f == pl.num_programs(2) - 1)
    def _():
        o_ref[...] = acc_scr[...].astype(bf16)


def _ffn(lidx, xe, p, cap):
    tmf = min(TMF, cap)
    grid_spec = pltpu.PrefetchScalarGridSpec(
        num_scalar_prefetch=1,
        grid=(N_EXPERTS, cap // tmf, EXPERT_FF // TFF),
        in_specs=[pl.BlockSpec((None, tmf, D_MODEL), lambda e, r, f, l: (e, r, 0)),
                  pl.BlockSpec((None, None, D_MODEL, TFF), lambda e, r, f, l: (l[0], e, 0, f)),
                  pl.BlockSpec((None, None, D_MODEL, TFF), lambda e, r, f, l: (l[0], e, 0, f)),
                  pl.BlockSpec((None, None, TFF, D_MODEL), lambda e, r, f, l: (l[0], e, f, 0))],
        out_specs=pl.BlockSpec((None, tmf, D_MODEL), lambda e, r, f, l: (e, r, 0)),
        scratch_shapes=[pltpu.VMEM((tmf, D_MODEL), f32)],
    )
    return pl.pallas_call(
        _ffn_kernel,
        grid_spec=grid_spec,
        out_shape=jax.ShapeDtypeStruct((N_EXPERTS, cap, D_MODEL), bf16),
        compiler_params=_cparams(("parallel", "parallel", "arbitrary"), 48 * 1024 * 1024),
        name="ffn",
    )(lidx, xe, p["w_gate"], p["w_up"], p["w_down"])


def _combine_kernel(tbl_ref, x_ref, pos_ref, aff_ref, *rest, ntile, cap):
    slabs = rest[:N_EXPERTS]
    o_ref = rest[N_EXPERTS]
    k = pl.program_id(0)
    acc = x_ref[...]
    pos = pos_ref[...]
    aff = aff_ref[...]
    ji = lax.broadcasted_iota(i32, (TOK_TILE, SLAB), 1)
    for e in range(N_EXPERTS):
        base = _slab_base(tbl_ref[e * ntile + k], cap)
        onehot = (ji == pos[:, e:e + 1] - base).astype(bf16)
        acc = acc + aff[:, e:e + 1] * jnp.dot(onehot, slabs[e][0], preferred_element_type=f32)
    o_ref[...] = acc


def _slab_spec(e, ntile, cap):
    def imap(k, t):
        return (e, pl.multiple_of(_slab_base(t[e * ntile + k], cap), BF16_ROWS), 0)
    return pl.BlockSpec((pl.Element(1), pl.Element(SLAB), pl.Element(D_MODEL)), imap)


def _combine(tbl, x, pos_t, aff_t, ye):
    n = x.shape[0]
    ntile = n // TOK_TILE
    cap = ye.shape[1]
    kern = functools.partial(_combine_kernel, ntile=ntile, cap=cap)
    grid_spec = pltpu.PrefetchScalarGridSpec(
        num_scalar_prefetch=1,
        grid=(ntile,),
        in_specs=[pl.BlockSpec((TOK_TILE, D_MODEL), lambda k, t: (k, 0)),
                  pl.BlockSpec((TOK_TILE, N_EXPERTS), lambda k, t: (k, 0)),
                  pl.BlockSpec((TOK_TILE, N_EXPERTS), lambda k, t: (k, 0))]
        + [_slab_spec(e, ntile, cap) for e in range(N_EXPERTS)],
        out_specs=pl.BlockSpec((TOK_TILE, D_MODEL), lambda k, t: (k, 0)),
    )
    return pl.pallas_call(
        kern,
        grid_spec=grid_spec,
        out_shape=jax.ShapeDtypeStruct((n, D_MODEL), f32),
        compiler_params=_cparams(("parallel",), 40 * 1024 * 1024),
        name="combine",
    )(tbl, x, pos_t, aff_t, *([ye] * N_EXPERTS))


def _norm_kernel(x_ref, g_ref, o_ref):
    x = x_ref[...]
    o_ref[...] = x * lax.rsqrt(jnp.mean(x * x, axis=-1, keepdims=True) + NORM_EPS) * g_ref[...]


def _final_norm(x, g):
    t = x.shape[0]
    return pl.pallas_call(
        _norm_kernel,
        grid=(t // TM1,),
        in_specs=[pl.BlockSpec((TM1, D_MODEL), lambda i: (i, 0)), pl.BlockSpec((1, D_MODEL), lambda i: (0, 0))],
        out_specs=pl.BlockSpec((TM1, D_MODEL), lambda i: (i, 0)),
        out_shape=jax.ShapeDtypeStruct((t, D_MODEL), f32),
        compiler_params=_cparams(("parallel",)),
        name="final_norm",
    )(x, g)


def _block_diag_pairs(w):
    d = w.shape[0]
    w = w.reshape(d, LRU_BLOCKS // 2, 2, LRU_BLOCK, LRU_BLOCK)
    zero = jnp.zeros_like(w[:, :, 0])
    top = jnp.concatenate([w[:, :, 0], zero], axis=-1)
    bot = jnp.concatenate([zero, w[:, :, 1]], axis=-1)
    return jnp.concatenate([top, bot], axis=-2)


def _pair_rows(v):
    d = v.shape[0]
    rest = v.shape[3:]
    v = jnp.moveaxis(v.reshape((d, 2, N_PAIR, 2) + rest), 1, 2).reshape((d, N_PAIR, 4) + rest)
    return jnp.concatenate([v, jnp.zeros_like(v)], axis=2).reshape((d, DT_ROWS) + rest)


def _prepare(norm_mix, w_in, lru_conv_w, lru_conv_b, lru_wa, lru_ba, lru_wx, lru_bx, lru_lambda,
             ssd_conv_w, ssd_conv_b, ssd_dt_bias, ssd_a_log, ssd_d, ssd_norm, w_branch, w_out,
             norm_ffn, w_router, w_gate, w_up, w_down):
    c_dt = 2 * LRU_WIDTH + SSD_INNER + SSD_XBC
    n_dt = 2 * SSD_HEADS
    w_main = jnp.concatenate([w_in[:, :, :c_dt], w_in[:, :, c_dt + n_dt:]], axis=-1).astype(bf16)
    w_dt = jnp.swapaxes(w_in[:, :, c_dt:c_dt + n_dt], 1, 2).reshape(DEPTH, 2, SSD_HEADS, D_MODEL)
    w_dt = _pair_rows(w_dt).astype(bf16)

    ident = jnp.zeros((DEPTH, CONV_K, 1), f32)
    conv_w = jnp.concatenate([
        lru_conv_w, jnp.broadcast_to(ident, (DEPTH, CONV_K, LRU_WIDTH + SSD_INNER)),
        ssd_conv_w, jnp.broadcast_to(ident, (DEPTH, CONV_K, 2 * D_MODEL))], axis=-1)
    conv_b = jnp.concatenate([
        lru_conv_b, jnp.zeros((DEPTH, LRU_WIDTH + SSD_INNER), f32),
        ssd_conv_b, jnp.zeros((DEPTH, 2 * D_MODEL), f32)], axis=-1)[:, None, :]

    lru_w = jnp.concatenate([_block_diag_pairs(lru_wa[:, 0]), _block_diag_pairs(lru_wx[:, 0]),
                             _block_diag_pairs(lru_wa[:, 1]), _block_diag_pairs(lru_wx[:, 1])], axis=-1).astype(bf16)
    ncb = LRU_WIDTH // LANES

    def chan(v):
        return v.reshape(DEPTH, ncb, 1, LANES)

    lru_b = jnp.concatenate([chan(lru_ba[:, 0]), chan(lru_bx[:, 0]), chan(lru_ba[:, 1]), chan(lru_bx[:, 1])], axis=-1)
    lru_lam = lru_lambda.reshape(DEPTH, 2, ncb, LANES).transpose(0, 2, 1, 3)

    return dict(
        norm_mix=norm_mix[:, None, :],
        w_main=w_main, w_dt=w_dt, b_dt=_pair_rows(ssd_dt_bias)[:, :, None],
        conv_w=conv_w, conv_b=conv_b,
        lru_w=lru_w, lru_b=lru_b, lru_lam=lru_lam,
        ssd_alog=_pair_rows(ssd_a_log).reshape(DEPTH, N_PAIR, 8, 1),
        ssd_dskip=jnp.repeat(ssd_d, SSD_HEAD_DIM, axis=-1).reshape(DEPTH, N_PAIR, 1, LANES),
        w_br_a=w_branch[:, :LRU_WIDTH].astype(bf16), w_br_b=w_branch[:, LRU_WIDTH:].astype(bf16),
        w_out=w_out.astype(bf16),
        ssd_norm=ssd_norm[:, None, :], norm_ffn=norm_ffn[:, None, :],
        w_rt_hi=_split_hi(jnp.swapaxes(w_router, 1, 2)), w_rt_lo=_split_lo(jnp.swapaxes(w_router, 1, 2)),
        w_gate=w_gate.astype(bf16), w_up=w_up.astype(bf16), w_down=w_down.astype(bf16),
    )


def _split_hi(w):
    return w.astype(bf16)


def _split_lo(w):
    return (w - w.astype(bf16).astype(f32)).astype(bf16)


def _layer(lidx, x, p, seq_len):
    n = x.shape[0]
    cap = CAPACITY_FACTOR * n // N_EXPERTS
    u, lg, z, xbc, gr, dt = _inproj(lidx, x, p, seq_len)
    ya = _lru(lidx, u, lg, p, seq_len)
    ys = _ssd(lidx, xbc, dt, p, seq_len)
    xo, h, logits_t = _post(lidx, ya, ys, z, gr, x, p)
    aff, pos, excl = _route(logits_t, cap)
    tbl = excl[:, ::TOK_TILE].reshape(-1)
    xe = _dispatch(tbl, pos, h, cap)
    ye = _ffn(lidx, xe, p, cap)
    return _combine(tbl, xo, pos.T, aff.T, ye)


def kernel(x_prompt, x_sample, norm_mix, w_in, lru_conv_w, lru_conv_b, lru_wa, lru_ba, lru_wx, lru_bx, lru_lambda, ssd_conv_w, ssd_conv_b, ssd_dt_bias, ssd_a_log, ssd_d, ssd_norm, w_branch, w_out, norm_ffn, w_router, w_gate, w_up, w_down, norm_final):
    p = _prepare(norm_mix, w_in, lru_conv_w, lru_conv_b, lru_wa, lru_ba, lru_wx, lru_bx, lru_lambda,
                 ssd_conv_w, ssd_conv_b, ssd_dt_bias, ssd_a_log, ssd_d, ssd_norm, w_branch, w_out,
                 norm_ffn, w_router, w_gate, w_up, w_down)
    bp, lp, _ = x_prompt.shape
    bs, ls, _ = x_sample.shape

    def body(i, xs):
        lidx = jnp.reshape(i, (1,)).astype(i32)
        return (_layer(lidx, xs[0], p, lp), _layer(lidx, xs[1], p, ls))

    xp, xs = lax.fori_loop(0, DEPTH, body, (x_prompt.reshape(bp * lp, D_MODEL), x_sample.reshape(bs * ls, D_MODEL)))
    g = norm_final[None, :]
    return (_final_norm(xp, g).reshape(bp, lp, D_MODEL), _final_norm(xs, g).reshape(bs, ls, D_MODEL))
```

```python
import functools

import jax
import jax.numpy as jnp
from jax import lax
from jax.experimental import pallas as pl
from jax.experimental.pallas import tpu as pltpu

f32 = jnp.float32
bf16 = jnp.bfloat16
i32 = jnp.int32

D_MODEL = 1024
DEPTH = 4
LRU_WIDTH = 1024
LRU_BLOCKS = 16
LRU_BLOCK = LRU_WIDTH // LRU_BLOCKS
LRU_C = 8.0
CONV_K = 4
SSD_INNER = 2 * D_MODEL
SSD_HEAD_DIM = 64
SSD_HEADS = SSD_INNER // SSD_HEAD_DIM
SSD_GROUPS = 4
SSD_STATE = 128
SSD_XBC = SSD_INNER + 2 * SSD_GROUPS * SSD_STATE
N_EXPERTS = 16
EXPERT_FF = 2048
CAPACITY_FACTOR = 2
NORM_EPS = 1e-6

LANES = 128
BF16_ROWS = 16
VMEM_LIMIT = 56 * 1024 * 1024

TM1 = 1024
SUB1 = 256
TN1 = 512
HALO = BF16_ROWS
N_MAIN = 2 * LRU_WIDTH + SSD_INNER + SSD_XBC + 2 * D_MODEL
NJ1 = N_MAIN // TN1
N_PAIR = SSD_HEADS // 2
DT_ROWS = 8 * N_PAIR

SCAN_ROWS = 128
CHUNK = 128
SSD_GROUP = 8
TOK_TILE = 128
DISP_TILES = 8
SLAB = TOK_TILE + BF16_ROWS
SLAB_FEW = 48
TM5 = 512
SUB5 = 256
TMF = 1024
TFF = 512
PFX = 512
NEG = -1e30
LOG2_E = 1.4426950408889634


def _cparams(sem, vmem=None):
    return pltpu.CompilerParams(dimension_semantics=sem, vmem_limit_bytes=vmem)


def _sigmoid(x):
    return 0.5 * jnp.tanh(0.5 * x) + 0.5


def _inproj_kernel(l_ref, x_ref, xp_ref, xn_ref, g_ref, w_ref, cw_ref, cb_ref, wdt_ref, bdt_ref,
                   u_ref, lg_ref, z_ref, xbc_ref, gr_ref, dt_ref, h_scr, a_scr, *, tiles_per_seq):
    i = pl.program_id(0)
    j = pl.program_id(1)

    @pl.when(j == 0)
    def _():
        g = g_ref[...]

        def nrm(x):
            return x * lax.rsqrt(jnp.mean(x * x, axis=-1, keepdims=True) + NORM_EPS) * g

        it = i % tiles_per_seq
        top_ok = (it != 0).astype(f32)
        bot_ok = (it != tiles_per_seq - 1).astype(f32)
        h_scr[0:HALO, :] = (nrm(xp_ref[...]) * top_ok).astype(bf16)
        hm = nrm(x_ref[...]).astype(bf16)
        h_scr[HALO:HALO + TM1, :] = hm
        h_scr[HALO + TM1:, :] = (nrm(xn_ref[...]) * bot_ok).astype(bf16)
        raw = lax.dot_general(wdt_ref[...], hm, (((1,), (1,)), ((), ())), preferred_element_type=f32)
        dt_ref[...] = jax.nn.softplus(raw + bdt_ref[...])

    def plain():
        return jnp.dot(h_scr[HALO:HALO + TM1, :], w_ref[...], preferred_element_type=f32).astype(bf16)

    def conv(o_ref, act):
        cw = cw_ref[...]
        for s in range(TM1 // SUB1):
            r0 = s * SUB1
            buf = a_scr.at[s % 2]
            buf[...] = jnp.dot(h_scr[r0:r0 + SUB1 + 2 * HALO, :], w_ref[...], preferred_element_type=f32)
            out = cb_ref[...] + cw[0:1] * buf[HALO - 2:HALO - 2 + SUB1, :]
            out = out + cw[1:2] * buf[HALO - 1:HALO - 1 + SUB1, :]
            out = out + cw[2:3] * buf[HALO:HALO + SUB1, :]
            out = out + cw[3:4] * buf[HALO + 1:HALO + 1 + SUB1, :]
            o_ref[r0:r0 + SUB1, :] = act(out).astype(bf16)

    @pl.when(j < 2)
    def _():
        conv(u_ref, lambda v: v)

    @pl.when((j >= 2) & (j < 4))
    def _():
        lg_ref[...] = plain()

    @pl.when((j >= 4) & (j < 8))
    def _():
        z_ref[...] = plain()

    @pl.when((j >= 8) & (j < 14))
    def _():
        conv(xbc_ref, lambda v: v * _sigmoid(v))

    @pl.when(j >= 14)
    def _():
        gr_ref[...] = plain()


def _inproj(lidx, x, p, seq_len):
    t = x.shape[0]
    nrow = t // TM1
    hb = TM1 // HALO
    nhb = t // HALO
    kern = functools.partial(_inproj_kernel, tiles_per_seq=seq_len // TM1)

    def seg(lo, hi):
        return lambda i, j, l: (i, jnp.clip(j, lo, hi) - lo)

    grid_spec = pltpu.PrefetchScalarGridSpec(
        num_scalar_prefetch=1,
        grid=(nrow, NJ1),
        in_specs=[
            pl.BlockSpec((TM1, D_MODEL), lambda i, j, l: (i, 0)),
            pl.BlockSpec((HALO, D_MODEL), lambda i, j, l: (jnp.maximum(i * hb - 1, 0), 0)),
            pl.BlockSpec((HALO, D_MODEL), lambda i, j, l: (jnp.minimum((i + 1) * hb, nhb - 1), 0)),
            pl.BlockSpec((None, 1, D_MODEL), lambda i, j, l: (l[0], 0, 0)),
            pl.BlockSpec((None, D_MODEL, TN1), lambda i, j, l: (l[0], 0, j)),
            pl.BlockSpec((None, CONV_K, TN1), lambda i, j, l: (l[0], 0, j)),
            pl.BlockSpec((None, 1, TN1), lambda i, j, l: (l[0], 0, j)),
            pl.BlockSpec((None, DT_ROWS, D_MODEL), lambda i, j, l: (l[0], 0, 0)),
            pl.BlockSpec((None, DT_ROWS, 1), lambda i, j, l: (l[0], 0, 0)),
        ],
        out_specs=[
            pl.BlockSpec((TM1, TN1), seg(0, 1)),
            pl.BlockSpec((TM1, TN1), seg(2, 3)),
            pl.BlockSpec((TM1, TN1), seg(4, 7)),
            pl.BlockSpec((TM1, TN1), seg(8, 13)),
            pl.BlockSpec((TM1, TN1), seg(14, 17)),
            pl.BlockSpec((DT_ROWS, TM1), lambda i, j, l: (0, i)),
        ],
        scratch_shapes=[pltpu.VMEM((TM1 + 2 * HALO, D_MODEL), bf16),
                        pltpu.VMEM((2, SUB1 + 2 * HALO, TN1), f32)],
    )
    return pl.pallas_call(
        kern,
        grid_spec=grid_spec,
        out_shape=[
            jax.ShapeDtypeStruct((t, LRU_WIDTH), bf16),
            jax.ShapeDtypeStruct((t, LRU_WIDTH), bf16),
            jax.ShapeDtypeStruct((t, SSD_INNER), bf16),
            jax.ShapeDtypeStruct((t, SSD_XBC), bf16),
            jax.ShapeDtypeStruct((t, 2 * D_MODEL), bf16),
            jax.ShapeDtypeStruct((DT_ROWS, t), f32),
        ],
        compiler_params=_cparams(("parallel", "arbitrary"), 40 * 1024 * 1024),
        name="inproj",
    )(lidx, x, x, x, p["norm_mix"], p["w_main"], p["conv_w"], p["conv_b"], p["w_dt"], p["b_dt"])


def _lru_kernel(l_ref, u_ref, g_ref, w_ref, b_ref, lam_ref, o_ref, hf_scr, *, seq_len):
    r = SCAN_ROWS
    nsub = seq_len // r
    ngrp = r // 8
    c8 = (LRU_C * LOG2_E) * jax.nn.log_sigmoid(lam_ref[...])
    row = lax.broadcasted_iota(i32, (r, LANES), 0) & 7

    def gates(j, d):
        off = pl.multiple_of(j * r, r)
        u = u_ref[pl.ds(off, r), :]
        g = jnp.dot(u, w_ref[:, d * 2 * LANES:(d + 1) * 2 * LANES], preferred_element_type=f32)
        g = g + b_ref[:, d * 2 * LANES:(d + 1) * 2 * LANES]
        rg = _sigmoid(g[:, :LANES])
        ig = _sigmoid(g[:, LANES:])
        a = jnp.exp2(c8[d:d + 1] * rg)
        v = 1.0 - a * a
        b = jnp.where(v > 0.0, v * lax.rsqrt(v), 0.0) * (ig * u.astype(f32))
        return off, a, b

    def scan8(a, b, reverse):
        for d in (1, 2, 4):
            if reverse:
                m = row <= 7 - d
                sh = r - d
            else:
                m = row >= d
                sh = d
            a_s = jnp.where(m, pltpu.roll(a, sh, 0), 1.0)
            b_s = jnp.where(m, pltpu.roll(b, sh, 0), 0.0)
            b = b + a * b_s
            a = a * a_s
        return a, b

    def carry(a, b, c, reverse):
        hs = [None] * ngrp
        order = range(ngrp - 1, -1, -1) if reverse else range(ngrp)
        for k in order:
            h = b[8 * k:8 * k + 8] + a[8 * k:8 * k + 8] * c
            c = h[0:1] if reverse else h[7:8]
            hs[k] = h
        return jnp.concatenate(hs, axis=0), c

    def fwd(j, c):
        off, a, b = gates(j, 0)
        a, b = scan8(a, b, False)
        h, c = carry(a, b, c, False)
        hf_scr[pl.ds(off, r), :] = h
        return c

    lax.fori_loop(0, nsub, fwd, jnp.zeros((1, LANES), f32), unroll=4)

    def bwd(jj, c):
        off, a, b = gates(nsub - 1 - jj, 1)
        a, b = scan8(a, b, True)
        h, c = carry(a, b, c, True)
        gate = jax.nn.gelu(g_ref[pl.ds(off, r), :].astype(f32))
        o_ref[pl.ds(off, r), :] = ((hf_scr[pl.ds(off, r), :] + h) * gate).astype(bf16)
        return c

    lax.fori_loop(0, nsub, bwd, jnp.zeros((1, LANES), f32), unroll=4)


def _lru(lidx, u, lg, p, seq_len):
    t = u.shape[0]
    nb = t // seq_len
    ncb = LRU_WIDTH // LANES
    kern = functools.partial(_lru_kernel, seq_len=seq_len)
    grid_spec = pltpu.PrefetchScalarGridSpec(
        num_scalar_prefetch=1,
        grid=(nb, ncb),
        in_specs=[
            pl.BlockSpec((seq_len, LANES), lambda b, c, l: (b, c)),
            pl.BlockSpec((seq_len, LANES), lambda b, c, l: (b, c)),
            pl.BlockSpec((None, None, LANES, 4 * LANES), lambda b, c, l: (l[0], c, 0, 0)),
            pl.BlockSpec((None, None, 1, 4 * LANES), lambda b, c, l: (l[0], c, 0, 0)),
            pl.BlockSpec((None, None, 2, LANES), lambda b, c, l: (l[0], c, 0, 0)),
        ],
        out_specs=pl.BlockSpec((seq_len, LANES), lambda b, c, l: (b, c)),
        scratch_shapes=[pltpu.VMEM((seq_len, LANES), f32)],
    )
    return pl.pallas_call(
        kern,
        grid_spec=grid_spec,
        out_shape=jax.ShapeDtypeStruct((t, LRU_WIDTH), bf16),
        compiler_params=_cparams(("parallel", "parallel"), VMEM_LIMIT),
        name="lru",
    )(lidx, u, lg, p["lru_w"], p["lru_b"], p["lru_lam"])


def _ssd_kernel(l_ref, x_ref, b_ref, c_ref, dt_ref, alog_ref, dsk_ref, o_ref,
                yp_scr, ub_scr, cs_scr, tc_scr, ef_scr, uf_scr, *, seq_len):
    q = CHUNK
    grp = SSD_GROUP
    gl = grp * q
    ngrp = seq_len // gl
    li = lax.broadcasted_iota(i32, (q, q), 0)
    si = lax.broadcasted_iota(i32, (q, q), 1)
    tril = si <= li
    triu = si >= li
    eye = (si == li).astype(bf16)
    m_l = lax.broadcasted_iota(i32, (q, LANES), 1) < SSD_HEAD_DIM
    a_col = -jnp.exp(alog_ref[...])

    def tables(g, carry):
        off = pl.multiple_of(g * gl, gl)
        lane = lax.broadcasted_iota(i32, (8, gl), 1) & (q - 1)
        fwd_rows = lax.broadcasted_iota(i32, (8, gl), 0) < 2
        dt = dt_ref[:, pl.ds(off, gl)]
        pre = dt * (a_col * LOG2_E)
        suf = pre
        for d in (1, 2, 4, 8, 16, 32, 64):
            pre = pre + jnp.where(lane >= d, pltpu.roll(pre, d, 1), 0.0)
            suf = suf + jnp.where(lane < q - d, pltpu.roll(suf, gl - d, 1), 0.0)
        cs = jnp.where(fwd_rows, pre, suf)
        cs_scr[:, pl.ds(off, gl)] = cs
        stack = jnp.concatenate([cs[:, c * q:(c + 1) * q] for c in range(grp)]
                                + [jnp.zeros((q - 8 * grp, q), f32)], axis=0)
        tc_scr[g] = stack.T
        return carry

    lax.fori_loop(0, ngrp, tables, 0, unroll=min(4, ngrp))

    nt = (((1,), (1,)), ((), ()))

    def fwd(g, s_f):
        tcol = tc_scr[g]
        offs = [pl.multiple_of(g * gl + c * q, q) for c in range(grp)]
        for c in range(grp):
            off = offs[c]
            x = x_ref[pl.ds(off, q), :]
            xl = jnp.where(m_l, x, jnp.zeros_like(x))
            xr = jnp.where(m_l, jnp.zeros_like(x), x)
            bc = b_ref[pl.ds(off, q), :]
            cs = cs_scr[:, pl.ds(off, q)]
            dt = dt_ref[:, pl.ds(off, q)]
            r = lax.dot_general(jnp.concatenate([c_ref[pl.ds(off, q), :], eye], axis=0), bc, nt,
                                preferred_element_type=f32)
            cb = r[:q]
            bt = r[q:]
            colb = [jnp.broadcast_to(tcol[:, 8 * c + k:8 * c + k + 1], (q, q)) for k in range(4)]
            acc = None
            for h in range(2):
                dec = jnp.exp2(jnp.where(tril, colb[h] - cs[h:h + 1, :], NEG)) * dt[h:h + 1, :]
                dec = dec + jnp.exp2(jnp.where(triu, colb[2 + h] - cs[2 + h:3 + h, :], NEG)) * dt[2 + h:3 + h, :]
                wf = dt[h:h + 1, :] * jnp.exp2(cs[h:h + 1, q - 1:q] - cs[h:h + 1, :])
                wb = dt[2 + h:3 + h, :] * jnp.exp2(cs[2 + h:3 + h, 0:1] - cs[2 + h:3 + h, :])
                lhs = jnp.concatenate([(cb * dec).astype(bf16), (bt * wf).astype(bf16), (bt * wb).astype(bf16)], axis=0)
                part = jnp.dot(lhs, xl if h == 0 else xr, preferred_element_type=f32)
                acc = part if acc is None else acc + part
            yp_scr[pl.ds(off, q), :] = acc[:q]
            uf_scr[c] = acc[q:2 * q]
            ub_scr[pl.ds(off, q), :] = acc[2 * q:]
            ef_scr[c] = jnp.exp2(jnp.where(m_l, colb[0], colb[1]))
        for c in range(grp):
            off = offs[c]
            ef = ef_scr[c]
            y = jnp.dot(c_ref[pl.ds(off, q), :], s_f.astype(bf16), preferred_element_type=f32) * ef
            yp_scr[pl.ds(off, q), :] = yp_scr[pl.ds(off, q), :] + y
            s_f = s_f * ef[q - 1:q, :] + uf_scr[c]
        return s_f

    lax.fori_loop(0, ngrp, fwd, jnp.zeros((SSD_STATE, LANES), f32))

    def bwd(gr, s_b):
        g = ngrp - 1 - gr
        tcol = tc_scr[g]
        for c in range(grp - 1, -1, -1):
            off = pl.multiple_of(g * gl + c * q, q)
            eb = jnp.exp2(jnp.where(m_l, jnp.broadcast_to(tcol[:, 8 * c + 2:8 * c + 3], (q, q)),
                                   jnp.broadcast_to(tcol[:, 8 * c + 3:8 * c + 4], (q, q))))
            y = jnp.dot(c_ref[pl.ds(off, q), :], s_b.astype(bf16), preferred_element_type=f32) * eb
            s_b = s_b * eb[0:1, :] + ub_scr[pl.ds(off, q), :]
            x = x_ref[pl.ds(off, q), :].astype(f32)
            o_ref[pl.ds(off, q), :] = (yp_scr[pl.ds(off, q), :] + y + dsk_ref[...] * x).astype(bf16)
        return s_b

    lax.fori_loop(0, ngrp, bwd, jnp.zeros((SSD_STATE, LANES), f32))


def _ssd(lidx, xbc, dt, p, seq_len):
    t = xbc.shape[0]
    nb = t // seq_len
    kern = functools.partial(_ssd_kernel, seq_len=seq_len)
    nxb = SSD_INNER // LANES
    pairs_per_group = N_PAIR // SSD_GROUPS
    grid_spec = pltpu.PrefetchScalarGridSpec(
        num_scalar_prefetch=1,
        grid=(nb, N_PAIR),
        in_specs=[
            pl.BlockSpec((seq_len, LANES), lambda b, h, l: (b, h)),
            pl.BlockSpec((seq_len, LANES), lambda b, h, l: (b, nxb + h // pairs_per_group)),
            pl.BlockSpec((seq_len, LANES), lambda b, h, l: (b, nxb + SSD_GROUPS + h // pairs_per_group)),
            pl.BlockSpec((8, seq_len), lambda b, h, l: (h, b)),
            pl.BlockSpec((None, None, 8, 1), lambda b, h, l: (l[0], h, 0, 0)),
            pl.BlockSpec((None, None, 1, LANES), lambda b, h, l: (l[0], h, 0, 0)),
        ],
        out_specs=pl.BlockSpec((seq_len, LANES), lambda b, h, l: (b, h)),
        scratch_shapes=[pltpu.VMEM((seq_len, LANES), f32),
                        pltpu.VMEM((seq_len, LANES), f32),
                        pltpu.VMEM((8, seq_len), f32),
                        pltpu.VMEM((seq_len // (SSD_GROUP * CHUNK), CHUNK, LANES), f32),
                        pltpu.VMEM((SSD_GROUP, CHUNK, LANES), f32),
                        pltpu.VMEM((SSD_GROUP, SSD_STATE, LANES), f32)],
    )
    return pl.pallas_call(
        kern,
        grid_spec=grid_spec,
        out_shape=jax.ShapeDtypeStruct((t, SSD_INNER), bf16),
        compiler_params=_cparams(("parallel", "parallel"), VMEM_LIMIT),
        name="ssd",
    )(lidx, xbc, xbc, xbc, dt, p["ssd_alog"], p["ssd_dskip"])


def _post_kernel(l_ref, ya_ref, ys_ref, z_ref, gr_ref, x_ref, wa_ref, wb_ref, wo_ref, gs_ref, gf_ref,
                 rh_ref, rl_ref, xo_ref, h_ref, lg_ref):
    nt = (((1,), (1,)), ((), ()))
    for s in range(TM5 // SUB5):
        rs = slice(s * SUB5, (s + 1) * SUB5)
        z = z_ref[rs, :].astype(f32)
        ysg = ys_ref[rs, :].astype(f32) * (z * _sigmoid(z))
        yb = ysg * lax.rsqrt(jnp.mean(ysg * ysg, axis=-1, keepdims=True) + NORM_EPS) * gs_ref[...]
        p_a = jnp.dot(ya_ref[rs, :], wa_ref[...], preferred_element_type=f32)
        p_b = jnp.dot(yb.astype(bf16), wb_ref[...], preferred_element_type=f32)
        gates = _sigmoid(gr_ref[rs, :].astype(f32))
        merged = gates[:, :D_MODEL] * p_a + gates[:, D_MODEL:] * p_b
        xo = x_ref[rs, :] + jnp.dot(merged.astype(bf16), wo_ref[...], preferred_element_type=f32)
        xo_ref[rs, :] = xo
        h = xo * lax.rsqrt(jnp.mean(xo * xo, axis=-1, keepdims=True) + NORM_EPS) * gf_ref[...]
        h_hi = h.astype(bf16)
        h_lo = (h - h_hi.astype(f32)).astype(bf16)
        h_ref[rs, :] = h_hi
        lg = lax.dot_general(rh_ref[...], h_hi, nt, preferred_element_type=f32)
        lg = lg + lax.dot_general(rh_ref[...], h_lo, nt, preferred_element_type=f32)
        lg = lg + lax.dot_general(rl_ref[...], h_hi, nt, preferred_element_type=f32)
        lg_ref[:, rs] = lg


def _post(lidx, ya, ys, z, gr, x, p):
    t = x.shape[0]

    def wspec(shape):
        return pl.BlockSpec((None,) + shape, lambda i, l: (l[0],) + (0,) * len(shape))

    def rows(width):
        return pl.BlockSpec((TM5, width), lambda i, l: (i, 0))

    grid_spec = pltpu.PrefetchScalarGridSpec(
        num_scalar_prefetch=1,
        grid=(t // TM5,),
        in_specs=[rows(LRU_WIDTH), rows(SSD_INNER), rows(SSD_INNER), rows(2 * D_MODEL), rows(D_MODEL),
                  wspec((LRU_WIDTH, D_MODEL)), wspec((SSD_INNER, D_MODEL)), wspec((D_MODEL, D_MODEL)),
                  wspec((1, SSD_INNER)), wspec((1, D_MODEL)),
                  wspec((N_EXPERTS, D_MODEL)), wspec((N_EXPERTS, D_MODEL))],
        out_specs=[rows(D_MODEL), rows(D_MODEL), pl.BlockSpec((N_EXPERTS, TM5), lambda i, l: (0, i))],
    )
    return pl.pallas_call(
        _post_kernel,
        grid_spec=grid_spec,
        out_shape=[jax.ShapeDtypeStruct((t, D_MODEL), f32),
                   jax.ShapeDtypeStruct((t, D_MODEL), bf16),
                   jax.ShapeDtypeStruct((N_EXPERTS, t), f32)],
        compiler_params=_cparams(("parallel",), 48 * 1024 * 1024),
        name="post",
    )(lidx, ya, ys, z, gr, x, p["w_br_a"], p["w_br_b"], p["w_out"], p["ssd_norm"], p["norm_ffn"],
      p["w_rt_hi"], p["w_rt_lo"])


def _route_kernel(lg_ref, aff_ref, pos_ref, excl_ref, bits_scr, *, cap):
    n = lg_ref.shape[1]
    lg = lg_ref[...]
    ex = jnp.exp(lg - jnp.max(lg, axis=0, keepdims=True))
    aff = ex / jnp.sum(ex, axis=0, keepdims=True)
    aff_ref[...] = aff
    bits_scr[...] = lax.bitcast_convert_type(aff, i32)

    def body(t, prefix):
        cand = prefix | jnp.left_shift(jnp.int32(1), 30 - t)
        cnt = jnp.sum((bits_scr[...] >= cand).astype(f32), axis=1, keepdims=True)
        return jnp.where(cnt >= cap, cand, prefix)

    thr = lax.fori_loop(0, 31, body, jnp.zeros((N_EXPERTS, 1), i32))
    bits = bits_scr[...]
    gt = (bits > thr).astype(f32)
    eq = (bits == thr).astype(f32)
    need = cap - jnp.sum(gt, axis=1, keepdims=True)

    ri = lax.broadcasted_iota(i32, (PFX, PFX), 0)
    ci = lax.broadcasted_iota(i32, (PFX, PFX), 1)
    upper = (ri <= ci).astype(bf16)

    def prefix_count(mask):
        outs = []
        carry = jnp.zeros((N_EXPERTS, 1), f32)
        for k in range(n // PFX):
            blk = mask[:, k * PFX:(k + 1) * PFX].astype(bf16)
            inc = jnp.dot(blk, upper, preferred_element_type=f32) + carry
            carry = inc[:, PFX - 1:PFX]
            outs.append(inc)
        return jnp.concatenate(outs, axis=1)

    sel = gt + eq * (prefix_count(eq) <= need).astype(f32)
    excl = prefix_count(sel) - sel
    excl_ref[...] = excl.astype(i32)
    pos_ref[...] = jnp.where(sel > 0.5, excl, -1.0).astype(i32)


def _route(logits_t, cap):
    n = logits_t.shape[1]
    kern = functools.partial(_route_kernel, cap=cap)
    full = pl.BlockSpec((N_EXPERTS, n), lambda i: (0, 0))
    return pl.pallas_call(
        kern,
        grid=(1,),
        in_specs=[full],
        out_specs=[full, full, full],
        out_shape=[jax.ShapeDtypeStruct((N_EXPERTS, n), f32),
                   jax.ShapeDtypeStruct((N_EXPERTS, n), i32),
                   jax.ShapeDtypeStruct((N_EXPERTS, n), i32)],
        scratch_shapes=[pltpu.VMEM((N_EXPERTS, n), i32)],
        compiler_params=_cparams(("arbitrary",), VMEM_LIMIT),
        name="route",
    )(logits_t)


def _slab_base(c0, cap):
    return jnp.minimum((c0 // BF16_ROWS) * BF16_ROWS, cap - SLAB)


def _dispatch_kernel(tbl_ref, pos_ref, h_ref, xe_ref, *, ntile, cap):
    e = pl.program_id(0)
    k = pl.program_id(1)

    @pl.when(k == 0)
    def _():
        xe_ref[...] = jnp.zeros_like(xe_ref)

    pos = pos_ref[pl.ds(e, 1), :]
    t0 = e * ntile + k * DISP_TILES
    bases = [pl.multiple_of(_slab_base(tbl_ref[t0 + s], cap), BF16_ROWS) for s in range(DISP_TILES)]

    def place(rows):
        ji = lax.broadcasted_iota(i32, (rows, TOK_TILE), 0)
        for s in range(DISP_TILES):
            onehot = (ji == pos[:, s * TOK_TILE:(s + 1) * TOK_TILE] - bases[s]).astype(bf16)
            slab = jnp.dot(onehot, h_ref[s * TOK_TILE:(s + 1) * TOK_TILE, :], preferred_element_type=f32)
            xe_ref[pl.ds(bases[s], rows), :] = xe_ref[pl.ds(bases[s], rows), :] + slab.astype(bf16)

    few = None
    for s in range(DISP_TILES):
        t = k * DISP_TILES + s
        end = jnp.where(t + 1 < ntile, tbl_ref[e * ntile + jnp.minimum(t + 1, ntile - 1)], cap)
        ok = end - bases[s] <= SLAB_FEW
        few = ok if few is None else jnp.logical_and(few, ok)

    @pl.when(few)
    def _():
        place(SLAB_FEW)

    @pl.when(jnp.logical_not(few))
    def _():
        place(SLAB)


def _dispatch(tbl, pos, h, cap):
    n = h.shape[0]
    ntile = n // TOK_TILE
    kern = functools.partial(_dispatch_kernel, ntile=ntile, cap=cap)
    grid_spec = pltpu.PrefetchScalarGridSpec(
        num_scalar_prefetch=1,
        grid=(N_EXPERTS, ntile // DISP_TILES),
        in_specs=[pl.BlockSpec((N_EXPERTS, DISP_TILES * TOK_TILE), lambda e, k, t: (0, k)),
                  pl.BlockSpec((DISP_TILES * TOK_TILE, D_MODEL), lambda e, k, t: (k, 0))],
        out_specs=pl.BlockSpec((None, cap, D_MODEL), lambda e, k, t: (e, 0, 0)),
    )
    return pl.pallas_call(
        kern,
        grid_spec=grid_spec,
        out_shape=jax.ShapeDtypeStruct((N_EXPERTS, cap, D_MODEL), bf16),
        compiler_params=_cparams(("parallel", "arbitrary"), 40 * 1024 * 1024),
        name="dispatch",
    )(tbl, pos, h)


def _ffn_kernel(l_ref, x_ref, wg_ref, wu_ref, wd_ref, o_ref, acc_scr):
    f = pl.program_id(2)

    @pl.when(f == 0)
    def _():
        acc_scr[...] = jnp.zeros_like(acc_scr)

    x = x_ref[...]
    g = jnp.dot(x, wg_ref[...].astype(bf16), preferred_element_type=f32)
    u = jnp.dot(x, wu_ref[...].astype(bf16), preferred_element_type=f32)
    hid = (g * _sigmoid(g) * u).astype(bf16)
    acc_scr[...] += jnp.dot(hid, wd_ref[...].astype(bf16), preferred_element_type=f32)

    @pl.when(f == pl.num_programs(2) - 1)
    def _():
        o_ref[...] = acc_scr[...].astype(bf16)


def _ffn(lidx, xe, p, cap):
    tmf = min(TMF, cap)
    grid_spec = pltpu.PrefetchScalarGridSpec(
        num_scalar_prefetch=1,
        grid=(N_EXPERTS, cap // tmf, EXPERT_FF // TFF),
        in_specs=[pl.BlockSpec((None, tmf, D_MODEL), lambda e, r, f, l: (e, r, 0)),
                  pl.BlockSpec((None, None, D_MODEL, TFF), lambda e, r, f, l: (l[0], e, 0, f)),
                  pl.BlockSpec((None, None, D_MODEL, TFF), lambda e, r, f, l: (l[0], e, 0, f)),
                  pl.BlockSpec((None, None, TFF, D_MODEL), lambda e, r, f, l: (l[0], e, f, 0))],
        out_specs=pl.BlockSpec((None, tmf, D_MODEL), lambda e, r, f, l: (e, r, 0)),
        scratch_shapes=[pltpu.VMEM((tmf, D_MODEL), f32)],
    )
    return pl.pallas_call(
        _ffn_kernel,
        grid_spec=grid_spec,
        out_shape=jax.ShapeDtypeStruct((N_EXPERTS, cap, D_MODEL), bf16),
        compiler_params=_cparams(("parallel", "parallel", "arbitrary"), 48 * 1024 * 1024),
        name="ffn",
    )(lidx, xe, p["w_gate"], p["w_up"], p["w_down"])


def _combine_kernel(tbl_ref, x_ref, pos_ref, aff_ref, *rest, ntile, cap):
    slabs = rest[:N_EXPERTS]
    o_ref = rest[N_EXPERTS]
    k = pl.program_id(0)
    acc = x_ref[...]
    pos = pos_ref[...]
    aff = aff_ref[...]
    ji = lax.broadcasted_iota(i32, (TOK_TILE, SLAB), 1)
    for e in range(N_EXPERTS):
        base = _slab_base(tbl_ref[e * ntile + k], cap)
        onehot = (ji == pos[:, e:e + 1] - base).astype(bf16)
        acc = acc + aff[:, e:e + 1] * jnp.dot(onehot, slabs[e][0], preferred_element_type=f32)
    o_ref[...] = acc


def _slab_spec(e, ntile, cap):
    def imap(k, t):
        return (e, pl.multiple_of(_slab_base(t[e * ntile + k], cap), BF16_ROWS), 0)
    return pl.BlockSpec((pl.Element(1), pl.Element(SLAB), pl.Element(D_MODEL)), imap)


def _combine(tbl, x, pos_t, aff_t, ye):
    n = x.shape[0]
    ntile = n // TOK_TILE
    cap = ye.shape[1]
    kern = functools.partial(_combine_kernel, ntile=ntile, cap=cap)
    grid_spec = pltpu.PrefetchScalarGridSpec(
        num_scalar_prefetch=1,
        grid=(ntile,),
        in_specs=[pl.BlockSpec((TOK_TILE, D_MODEL), lambda k, t: (k, 0)),
                  pl.BlockSpec((TOK_TILE, N_EXPERTS), lambda k, t: (k, 0)),
                  pl.BlockSpec((TOK_TILE, N_EXPERTS), lambda k, t: (k, 0))]
        + [_slab_spec(e, ntile, cap) for e in range(N_EXPERTS)],
        out_specs=pl.BlockSpec((TOK_TILE, D_MODEL), lambda k, t: (k, 0)),
    )
    return pl.pallas_call(
        kern,
        grid_spec=grid_spec,
        out_shape=jax.ShapeDtypeStruct((n, D_MODEL), f32),
        compiler_params=_cparams(("parallel",), 40 * 1024 * 1024),
        name="combine",
    )(tbl, x, pos_t, aff_t, *([ye] * N_EXPERTS))


def _norm_kernel(x_ref, g_ref, o_ref):
    x = x_ref[...]
    o_ref[...] = x * lax.rsqrt(jnp.mean(x * x, axis=-1, keepdims=True) + NORM_EPS) * g_ref[...]


def _final_norm(x, g):
    t = x.shape[0]
    return pl.pallas_call(
        _norm_kernel,
        grid=(t // TM1,),
        in_specs=[pl.BlockSpec((TM1, D_MODEL), lambda i: (i, 0)), pl.BlockSpec((1, D_MODEL), lambda i: (0, 0))],
        out_specs=pl.BlockSpec((TM1, D_MODEL), lambda i: (i, 0)),
        out_shape=jax.ShapeDtypeStruct((t, D_MODEL), f32),
        compiler_params=_cparams(("parallel",)),
        name="final_norm",
    )(x, g)


def _block_diag_pairs(w):
    d = w.shape[0]
    w = w.reshape(d, LRU_BLOCKS // 2, 2, LRU_BLOCK, LRU_BLOCK)
    zero = jnp.zeros_like(w[:, :, 0])
    top = jnp.concatenate([w[:, :, 0], zero], axis=-1)
    bot = jnp.concatenate([zero, w[:, :, 1]], axis=-1)
    return jnp.concatenate([top, bot], axis=-2)


def _pair_rows(v):
    d = v.shape[0]
    rest = v.shape[3:]
    v = jnp.moveaxis(v.reshape((d, 2, N_PAIR, 2) + rest), 1, 2).reshape((d, N_PAIR, 4) + rest)
    return jnp.concatenate([v, jnp.zeros_like(v)], axis=2).reshape((d, DT_ROWS) + rest)


def _prepare(norm_mix, w_in, lru_conv_w, lru_conv_b, lru_wa, lru_ba, lru_wx, lru_bx, lru_lambda,
             ssd_conv_w, ssd_conv_b, ssd_dt_bias, ssd_a_log, ssd_d, ssd_norm, w_branch, w_out,
             norm_ffn, w_router, w_gate, w_up, w_down):
    c_dt = 2 * LRU_WIDTH + SSD_INNER + SSD_XBC
    n_dt = 2 * SSD_HEADS
    w_main = jnp.concatenate([w_in[:, :, :c_dt], w_in[:, :, c_dt + n_dt:]], axis=-1).astype(bf16)
    w_dt = jnp.swapaxes(w_in[:, :, c_dt:c_dt + n_dt], 1, 2).reshape(DEPTH, 2, SSD_HEADS, D_MODEL)
    w_dt = _pair_rows(w_dt).astype(bf16)

    ident = jnp.zeros((DEPTH, CONV_K, 1), f32)
    conv_w = jnp.concatenate([
        lru_conv_w, jnp.broadcast_to(ident, (DEPTH, CONV_K, LRU_WIDTH + SSD_INNER)),
        ssd_conv_w, jnp.broadcast_to(ident, (DEPTH, CONV_K, 2 * D_MODEL))], axis=-1)
    conv_b = jnp.concatenate([
        lru_conv_b, jnp.zeros((DEPTH, LRU_WIDTH + SSD_INNER), f32),
        ssd_conv_b, jnp.zeros((DEPTH, 2 * D_MODEL), f32)], axis=-1)[:, None, :]

    lru_w = jnp.concatenate([_block_diag_pairs(lru_wa[:, 0]), _block_diag_pairs(lru_wx[:, 0]),
                             _block_diag_pairs(lru_wa[:, 1]), _block_diag_pairs(lru_wx[:, 1])], axis=-1).astype(bf16)
    ncb = LRU_WIDTH // LANES

    def chan(v):
        return v.reshape(DEPTH, ncb, 1, LANES)

    lru_b = jnp.concatenate([chan(lru_ba[:, 0]), chan(lru_bx[:, 0]), chan(lru_ba[:, 1]), chan(lru_bx[:, 1])], axis=-1)
    lru_lam = lru_lambda.reshape(DEPTH, 2, ncb, LANES).transpose(0, 2, 1, 3)

    return dict(
        norm_mix=norm_mix[:, None, :],
        w_main=w_main, w_dt=w_dt, b_dt=_pair_rows(ssd_dt_bias)[:, :, None],
        conv_w=conv_w, conv_b=conv_b,
        lru_w=lru_w, lru_b=lru_b, lru_lam=lru_lam,
        ssd_alog=_pair_rows(ssd_a_log).reshape(DEPTH, N_PAIR, 8, 1),
        ssd_dskip=jnp.repeat(ssd_d, SSD_HEAD_DIM, axis=-1).reshape(DEPTH, N_PAIR, 1, LANES),
        w_br_a=w_branch[:, :LRU_WIDTH].astype(bf16), w_br_b=w_branch[:, LRU_WIDTH:].astype(bf16),
        w_out=w_out.astype(bf16),
        ssd_norm=ssd_norm[:, None, :], norm_ffn=norm_ffn[:, None, :],
        w_rt_hi=_split_hi(jnp.swapaxes(w_router, 1, 2)), w_rt_lo=_split_lo(jnp.swapaxes(w_router, 1, 2)),
        w_gate=w_gate, w_up=w_up, w_down=w_down,
    )


def _split_hi(w):
    return w.astype(bf16)


def _split_lo(w):
    return (w - w.astype(bf16).astype(f32)).astype(bf16)


def _layer(lidx, x, p, seq_len):
    n = x.shape[0]
    cap = CAPACITY_FACTOR * n // N_EXPERTS
    u, lg, z, xbc, gr, dt = _inproj(lidx, x, p, seq_len)
    ya = _lru(lidx, u, lg, p, seq_len)
    ys = _ssd(lidx, xbc, dt, p, seq_len)
    xo, h, logits_t = _post(lidx, ya, ys, z, gr, x, p)
    aff, pos, excl = _route(logits_t, cap)
    tbl = excl[:, ::TOK_TILE].reshape(-1)
    xe = _dispatch(tbl, pos, h, cap)
    ye = _ffn(lidx, xe, p, cap)
    return _combine(tbl, xo, pos.T, aff.T, ye)


def kernel(x_prompt, x_sample, norm_mix, w_in, lru_conv_w, lru_conv_b, lru_wa, lru_ba, lru_wx, lru_bx, lru_lambda, ssd_conv_w, ssd_conv_b, ssd_dt_bias, ssd_a_log, ssd_d, ssd_norm, w_branch, w_out, norm_ffn, w_router, w_gate, w_up, w_down, norm_final):
    p = _prepare(norm_mix, w_in, lru_conv_w, lru_conv_b, lru_wa, lru_ba, lru_wx, lru_bx, lru_lambda,
                 ssd_conv_w, ssd_conv_b, ssd_dt_bias, ssd_a_log, ssd_d, ssd_norm, w_branch, w_out,
                 norm_ffn, w_router, w_gate, w_up, w_down)
    bp, lp, _ = x_prompt.shape
    bs, ls, _ = x_sample.shape

    def body(i, xs):
        lidx = jnp.reshape(i, (1,)).astype(i32)
        return (_layer(lidx, xs[0], p, lp), _layer(lidx, xs[1], p, ls))

    xp, xs = lax.fori_loop(0, DEPTH, body, (x_prompt.reshape(bp * lp, D_MODEL), x_sample.reshape(bs * ls, D_MODEL)))
    g = norm_final[None, :]
    return (_final_norm(xp, g).reshape(bp, lp, D_MODEL), _final_norm(xs, g).reshape(bs, ls, D_MODEL))
```

```python
import functools

import jax
import jax.numpy as jnp
from jax import lax
from jax.experimental import pallas as pl
from jax.experimental.pallas import tpu as pltpu

f32 = jnp.float32
bf16 = jnp.bfloat16
i32 = jnp.int32

D_MODEL = 1024
DEPTH = 4
LRU_WIDTH = 1024
LRU_BLOCKS = 16
LRU_BLOCK = LRU_WIDTH // LRU_BLOCKS
LRU_C = 8.0
CONV_K = 4
SSD_INNER = 2 * D_MODEL
SSD_HEAD_DIM = 64
SSD_HEADS = SSD_INNER // SSD_HEAD_DIM
SSD_GROUPS = 4
SSD_STATE = 128
SSD_XBC = SSD_INNER + 2 * SSD_GROUPS * SSD_STATE
N_EXPERTS = 16
EXPERT_FF = 2048
CAPACITY_FACTOR = 2
NORM_EPS = 1e-6

LANES = 128
BF16_ROWS = 16
VMEM_LIMIT = 56 * 1024 * 1024

TM1 = 1024
SUB1 = 256
TN1 = 512
HALO = BF16_ROWS
N_MAIN = 2 * LRU_WIDTH + SSD_INNER + SSD_XBC + 2 * D_MODEL
NJ1 = N_MAIN // TN1
N_PAIR = SSD_HEADS // 2
DT_ROWS = 8 * N_PAIR

SCAN_ROWS = 128
CHUNK = 128
SSD_GROUP = 8
TOK_TILE = 128
DISP_TILES = 8
DISP_EXPERTS = 2
SLAB = TOK_TILE + BF16_ROWS
SLAB_FEW = 48
TM5 = 512
SUB5 = 256
TMF = 1024
TFF = 512
PFX = 512
NEG = -1e30
LOG2_E = 1.4426950408889634


def _cparams(sem, vmem=None):
    return pltpu.CompilerParams(dimension_semantics=sem, vmem_limit_bytes=vmem)


def _sigmoid(x):
    return 0.5 * jnp.tanh(0.5 * x) + 0.5


def _inproj_kernel(l_ref, x_ref, xp_ref, xn_ref, g_ref, w_ref, cw_ref, cb_ref, wdt_ref, bdt_ref,
                   u_ref, lg_ref, z_ref, xbc_ref, gr_ref, dt_ref, h_scr, a_scr, *, tiles_per_seq):
    i = pl.program_id(0)
    j = pl.program_id(1)

    @pl.when(j == 0)
    def _():
        g = g_ref[...]

        def nrm(x):
            return x * lax.rsqrt(jnp.mean(x * x, axis=-1, keepdims=True) + NORM_EPS) * g

        it = i % tiles_per_seq
        top_ok = (it != 0).astype(f32)
        bot_ok = (it != tiles_per_seq - 1).astype(f32)
        h_scr[0:HALO, :] = (nrm(xp_ref[...]) * top_ok).astype(bf16)
        hm = nrm(x_ref[...]).astype(bf16)
        h_scr[HALO:HALO + TM1, :] = hm
        h_scr[HALO + TM1:, :] = (nrm(xn_ref[...]) * bot_ok).astype(bf16)
        raw = lax.dot_general(wdt_ref[...], hm, (((1,), (1,)), ((), ())), preferred_element_type=f32)
        dt_ref[...] = jax.nn.softplus(raw + bdt_ref[...])

    def plain():
        return jnp.dot(h_scr[HALO:HALO + TM1, :], w_ref[...], preferred_element_type=f32).astype(bf16)

    def conv(o_ref, act):
        cw = cw_ref[...]
        for s in range(TM1 // SUB1):
            r0 = s * SUB1
            buf = a_scr.at[s % 2]
            buf[...] = jnp.dot(h_scr[r0:r0 + SUB1 + 2 * HALO, :], w_ref[...], preferred_element_type=f32)
            out = cb_ref[...] + cw[0:1] * buf[HALO - 2:HALO - 2 + SUB1, :]
            out = out + cw[1:2] * buf[HALO - 1:HALO - 1 + SUB1, :]
            out = out + cw[2:3] * buf[HALO:HALO + SUB1, :]
            out = out + cw[3:4] * buf[HALO + 1:HALO + 1 + SUB1, :]
            o_ref[r0:r0 + SUB1, :] = act(out).astype(bf16)

    @pl.when(j < 2)
    def _():
        conv(u_ref, lambda v: v)

    @pl.when((j >= 2) & (j < 4))
    def _():
        lg_ref[...] = plain()

    @pl.when((j >= 4) & (j < 8))
    def _():
        z_ref[...] = plain()

    @pl.when((j >= 8) & (j < 14))
    def _():
        conv(xbc_ref, lambda v: v * _sigmoid(v))

    @pl.when(j >= 14)
    def _():
        gr_ref[...] = plain()


def _inproj(lidx, x, p, seq_len):
    t = x.shape[0]
    nrow = t // TM1
    hb = TM1 // HALO
    nhb = t // HALO
    kern = functools.partial(_inproj_kernel, tiles_per_seq=seq_len // TM1)

    def seg(lo, hi):
        return lambda i, j, l: (i, jnp.clip(j, lo, hi) - lo)

    grid_spec = pltpu.PrefetchScalarGridSpec(
        num_scalar_prefetch=1,
        grid=(nrow, NJ1),
        in_specs=[
            pl.BlockSpec((TM1, D_MODEL), lambda i, j, l: (i, 0)),
            pl.BlockSpec((HALO, D_MODEL), lambda i, j, l: (jnp.maximum(i * hb - 1, 0), 0)),
            pl.BlockSpec((HALO, D_MODEL), lambda i, j, l: (jnp.minimum((i + 1) * hb, nhb - 1), 0)),
            pl.BlockSpec((None, 1, D_MODEL), lambda i, j, l: (l[0], 0, 0)),
            pl.BlockSpec((None, D_MODEL, TN1), lambda i, j, l: (l[0], 0, j)),
            pl.BlockSpec((None, CONV_K, TN1), lambda i, j, l: (l[0], 0, j)),
            pl.BlockSpec((None, 1, TN1), lambda i, j, l: (l[0], 0, j)),
            pl.BlockSpec((None, DT_ROWS, D_MODEL), lambda i, j, l: (l[0], 0, 0)),
            pl.BlockSpec((None, DT_ROWS, 1), lambda i, j, l: (l[0], 0, 0)),
        ],
        out_specs=[
            pl.BlockSpec((TM1, TN1), seg(0, 1)),
            pl.BlockSpec((TM1, TN1), seg(2, 3)),
            pl.BlockSpec((TM1, TN1), seg(4, 7)),
            pl.BlockSpec((TM1, TN1), seg(8, 13)),
            pl.BlockSpec((TM1, TN1), seg(14, 17)),
            pl.BlockSpec((DT_ROWS, TM1), lambda i, j, l: (0, i)),
        ],
        scratch_shapes=[pltpu.VMEM((TM1 + 2 * HALO, D_MODEL), bf16),
                        pltpu.VMEM((2, SUB1 + 2 * HALO, TN1), f32)],
    )
    return pl.pallas_call(
        kern,
        grid_spec=grid_spec,
        out_shape=[
            jax.ShapeDtypeStruct((t, LRU_WIDTH), bf16),
            jax.ShapeDtypeStruct((t, LRU_WIDTH), bf16),
            jax.ShapeDtypeStruct((t, SSD_INNER), bf16),
            jax.ShapeDtypeStruct((t, SSD_XBC), bf16),
            jax.ShapeDtypeStruct((t, 2 * D_MODEL), bf16),
            jax.ShapeDtypeStruct((DT_ROWS, t), f32),
        ],
        compiler_params=_cparams(("parallel", "arbitrary"), 40 * 1024 * 1024),
        name="inproj",
    )(lidx, x, x, x, p["norm_mix"], p["w_main"], p["conv_w"], p["conv_b"], p["w_dt"], p["b_dt"])


def _lru_kernel(l_ref, u_ref, g_ref, w_ref, b_ref, lam_ref, o_ref, hf_scr, *, seq_len):
    r = SCAN_ROWS
    nsub = seq_len // r
    ngrp = r // 8
    c8 = (LRU_C * LOG2_E) * jax.nn.log_sigmoid(lam_ref[...])
    row = lax.broadcasted_iota(i32, (r, LANES), 0) & 7

    def gates(j, d):
        off = pl.multiple_of(j * r, r)
        u = u_ref[pl.ds(off, r), :]
        g = jnp.dot(u, w_ref[:, d * 2 * LANES:(d + 1) * 2 * LANES], preferred_element_type=f32)
        g = g + b_ref[:, d * 2 * LANES:(d + 1) * 2 * LANES]
        rg = _sigmoid(g[:, :LANES])
        ig = _sigmoid(g[:, LANES:])
        a = jnp.exp2(c8[d:d + 1] * rg)
        v = 1.0 - a * a
        b = jnp.where(v > 0.0, v * lax.rsqrt(v), 0.0) * (ig * u.astype(f32))
        return off, a, b

    def scan8(a, b, reverse):
        for d in (1, 2, 4):
            if reverse:
                m = row <= 7 - d
                sh = r - d
            else:
                m = row >= d
                sh = d
            a_s = jnp.where(m, pltpu.roll(a, sh, 0), 1.0)
            b_s = jnp.where(m, pltpu.roll(b, sh, 0), 0.0)
            b = b + a * b_s
            a = a * a_s
        return a, b

    def carry(a, b, c, reverse):
        hs = [None] * ngrp
        order = range(ngrp - 1, -1, -1) if reverse else range(ngrp)
        for k in order:
            h = b[8 * k:8 * k + 8] + a[8 * k:8 * k + 8] * c
            c = h[0:1] if reverse else h[7:8]
            hs[k] = h
        return jnp.concatenate(hs, axis=0), c

    def fwd(j, c):
        off, a, b = gates(j, 0)
        a, b = scan8(a, b, False)
        h, c = carry(a, b, c, False)
        hf_scr[pl.ds(off, r), :] = h
        return c

    lax.fori_loop(0, nsub, fwd, jnp.zeros((1, LANES), f32), unroll=4)

    def bwd(jj, c):
        off, a, b = gates(nsub - 1 - jj, 1)
        a, b = scan8(a, b, True)
        h, c = carry(a, b, c, True)
        gate = jax.nn.gelu(g_ref[pl.ds(off, r), :].astype(f32))
        o_ref[pl.ds(off, r), :] = ((hf_scr[pl.ds(off, r), :] + h) * gate).astype(bf16)
        return c

    lax.fori_loop(0, nsub, bwd, jnp.zeros((1, LANES), f32), unroll=4)


def _lru(lidx, u, lg, p, seq_len):
    t = u.shape[0]
    nb = t // seq_len
    ncb = LRU_WIDTH // LANES
    kern = functools.partial(_lru_kernel, seq_len=seq_len)
    grid_spec = pltpu.PrefetchScalarGridSpec(
        num_scalar_prefetch=1,
        grid=(nb, ncb),
        in_specs=[
            pl.BlockSpec((seq_len, LANES), lambda b, c, l: (b, c)),
            pl.BlockSpec((seq_len, LANES), lambda b, c, l: (b, c)),
            pl.BlockSpec((None, None, LANES, 4 * LANES), lambda b, c, l: (l[0], c, 0, 0)),
            pl.BlockSpec((None, None, 1, 4 * LANES), lambda b, c, l: (l[0], c, 0, 0)),
            pl.BlockSpec((None, None, 2, LANES), lambda b, c, l: (l[0], c, 0, 0)),
        ],
        out_specs=pl.BlockSpec((seq_len, LANES), lambda b, c, l: (b, c)),
        scratch_shapes=[pltpu.VMEM((seq_len, LANES), f32)],
    )
    return pl.pallas_call(
        kern,
        grid_spec=grid_spec,
        out_shape=jax.ShapeDtypeStruct((t, LRU_WIDTH), bf16),
        compiler_params=_cparams(("parallel", "parallel"), VMEM_LIMIT),
        name="lru",
    )(lidx, u, lg, p["lru_w"], p["lru_b"], p["lru_lam"])


def _ssd_kernel(l_ref, x_ref, b_ref, c_ref, dt_ref, alog_ref, dsk_ref, o_ref,
                yp_scr, ub_scr, cs_scr, tc_scr, ef_scr, uf_scr, *, seq_len):
    q = CHUNK
    grp = SSD_GROUP
    gl = grp * q
    ngrp = seq_len // gl
    li = lax.broadcasted_iota(i32, (q, q), 0)
    si = lax.broadcasted_iota(i32, (q, q), 1)
    tril = si <= li
    triu = si >= li
    eye = (si == li).astype(bf16)
    m_l = lax.broadcasted_iota(i32, (q, LANES), 1) < SSD_HEAD_DIM
    a_col = -jnp.exp(alog_ref[...])

    def tables(g, carry):
        off = pl.multiple_of(g * gl, gl)
        lane = lax.broadcasted_iota(i32, (8, gl), 1) & (q - 1)
        fwd_rows = lax.broadcasted_iota(i32, (8, gl), 0) < 2
        dt = dt_ref[:, pl.ds(off, gl)]
        pre = dt * (a_col * LOG2_E)
        suf = pre
        for d in (1, 2, 4, 8, 16, 32, 64):
            pre = pre + jnp.where(lane >= d, pltpu.roll(pre, d, 1), 0.0)
            suf = suf + jnp.where(lane < q - d, pltpu.roll(suf, gl - d, 1), 0.0)
        cs = jnp.where(fwd_rows, pre, suf)
        cs_scr[:, pl.ds(off, gl)] = cs
        stack = jnp.concatenate([cs[:, c * q:(c + 1) * q] for c in range(grp)]
                                + [jnp.zeros((q - 8 * grp, q), f32)], axis=0)
        tc_scr[g] = stack.T
        return carry

    lax.fori_loop(0, ngrp, tables, 0, unroll=min(4, ngrp))

    nt = (((1,), (1,)), ((), ()))

    def fwd(g, s_f):
        tcol = tc_scr[g]
        offs = [pl.multiple_of(g * gl + c * q, q) for c in range(grp)]
        for c in range(grp):
            off = offs[c]
            x = x_ref[pl.ds(off, q), :]
            xl = jnp.where(m_l, x, jnp.zeros_like(x))
            xr = jnp.where(m_l, jnp.zeros_like(x), x)
            bc = b_ref[pl.ds(off, q), :]
            cs = cs_scr[:, pl.ds(off, q)]
            dt = dt_ref[:, pl.ds(off, q)]
            r = lax.dot_general(jnp.concatenate([c_ref[pl.ds(off, q), :], eye], axis=0), bc, nt,
                                preferred_element_type=f32)
            cb = r[:q]
            bt = r[q:]
            colb = [jnp.broadcast_to(tcol[:, 8 * c + k:8 * c + k + 1], (q, q)) for k in range(4)]
            acc = None
            for h in range(2):
                dec = jnp.exp2(jnp.where(tril, colb[h] - cs[h:h + 1, :], NEG)) * dt[h:h + 1, :]
                dec = dec + jnp.exp2(jnp.where(triu, colb[2 + h] - cs[2 + h:3 + h, :], NEG)) * dt[2 + h:3 + h, :]
                wf = dt[h:h + 1, :] * jnp.exp2(cs[h:h + 1, q - 1:q] - cs[h:h + 1, :])
                wb = dt[2 + h:3 + h, :] * jnp.exp2(cs[2 + h:3 + h, 0:1] - cs[2 + h:3 + h, :])
                lhs = jnp.concatenate([(cb * dec).astype(bf16), (bt * wf).astype(bf16), (bt * wb).astype(bf16)], axis=0)
                part = jnp.dot(lhs, xl if h == 0 else xr, preferred_element_type=f32)
                acc = part if acc is None else acc + part
            yp_scr[pl.ds(off, q), :] = acc[:q]
            uf_scr[c] = acc[q:2 * q]
            ub_scr[pl.ds(off, q), :] = acc[2 * q:]
            ef_scr[c] = jnp.exp2(jnp.where(m_l, colb[0], colb[1]))
        for c in range(grp):
            off = offs[c]
            ef = ef_scr[c]
            y = jnp.dot(c_ref[pl.ds(off, q), :], s_f.astype(bf16), preferred_element_type=f32) * ef
            yp_scr[pl.ds(off, q), :] = yp_scr[pl.ds(off, q), :] + y
            s_f = s_f * ef[q - 1:q, :] + uf_scr[c]
        return s_f

    lax.fori_loop(0, ngrp, fwd, jnp.zeros((SSD_STATE, LANES), f32))

    def bwd(gr, s_b):
        g = ngrp - 1 - gr
        tcol = tc_scr[g]
        for c in range(grp - 1, -1, -1):
            off = pl.multiple_of(g * gl + c * q, q)
            eb = jnp.exp2(jnp.where(m_l, jnp.broadcast_to(tcol[:, 8 * c + 2:8 * c + 3], (q, q)),
                                   jnp.broadcast_to(tcol[:, 8 * c + 3:8 * c + 4], (q, q))))
            y = jnp.dot(c_ref[pl.ds(off, q), :], s_b.astype(bf16), preferred_element_type=f32) * eb
            s_b = s_b * eb[0:1, :] + ub_scr[pl.ds(off, q), :]
            x = x_ref[pl.ds(off, q), :].astype(f32)
            o_ref[pl.ds(off, q), :] = (yp_scr[pl.ds(off, q), :] + y + dsk_ref[...] * x).astype(bf16)
        return s_b

    lax.fori_loop(0, ngrp, bwd, jnp.zeros((SSD_STATE, LANES), f32))


def _ssd(lidx, xbc, dt, p, seq_len):
    t = xbc.shape[0]
    nb = t // seq_len
    kern = functools.partial(_ssd_kernel, seq_len=seq_len)
    nxb = SSD_INNER // LANES
    pairs_per_group = N_PAIR // SSD_GROUPS
    grid_spec = pltpu.PrefetchScalarGridSpec(
        num_scalar_prefetch=1,
        grid=(nb, N_PAIR),
        in_specs=[
            pl.BlockSpec((seq_len, LANES), lambda b, h, l: (b, h)),
            pl.BlockSpec((seq_len, LANES), lambda b, h, l: (b, nxb + h // pairs_per_group)),
            pl.BlockSpec((seq_len, LANES), lambda b, h, l: (b, nxb + SSD_GROUPS + h // pairs_per_group)),
            pl.BlockSpec((8, seq_len), lambda b, h, l: (h, b)),
            pl.BlockSpec((None, None, 8, 1), lambda b, h, l: (l[0], h, 0, 0)),
            pl.BlockSpec((None, None, 1, LANES), lambda b, h, l: (l[0], h, 0, 0)),
        ],
        out_specs=pl.BlockSpec((seq_len, LANES), lambda b, h, l: (b, h)),
        scratch_shapes=[pltpu.VMEM((seq_len, LANES), f32),
                        pltpu.VMEM((seq_len, LANES), f32),
                        pltpu.VMEM((8, seq_len), f32),
                        pltpu.VMEM((seq_len // (SSD_GROUP * CHUNK), CHUNK, LANES), f32),
                        pltpu.VMEM((SSD_GROUP, CHUNK, LANES), f32),
                        pltpu.VMEM((SSD_GROUP, SSD_STATE, LANES), f32)],
    )
    return pl.pallas_call(
        kern,
        grid_spec=grid_spec,
        out_shape=jax.ShapeDtypeStruct((t, SSD_INNER), bf16),
        compiler_params=_cparams(("parallel", "parallel"), VMEM_LIMIT),
        name="ssd",
    )(lidx, xbc, xbc, xbc, dt, p["ssd_alog"], p["ssd_dskip"])


def _post_kernel(l_ref, ya_ref, ys_ref, z_ref, gr_ref, x_ref, wa_ref, wb_ref, wo_ref, gs_ref, gf_ref,
                 rh_ref, rl_ref, xo_ref, h_ref, lg_ref):
    nt = (((1,), (1,)), ((), ()))
    for s in range(TM5 // SUB5):
        rs = slice(s * SUB5, (s + 1) * SUB5)
        z = z_ref[rs, :].astype(f32)
        ysg = ys_ref[rs, :].astype(f32) * (z * _sigmoid(z))
        yb = ysg * lax.rsqrt(jnp.mean(ysg * ysg, axis=-1, keepdims=True) + NORM_EPS) * gs_ref[...]
        p_a = jnp.dot(ya_ref[rs, :], wa_ref[...], preferred_element_type=f32)
        p_b = jnp.dot(yb.astype(bf16), wb_ref[...], preferred_element_type=f32)
        gates = _sigmoid(gr_ref[rs, :].astype(f32))
        merged = gates[:, :D_MODEL] * p_a + gates[:, D_MODEL:] * p_b
        xo = x_ref[rs, :] + jnp.dot(merged.astype(bf16), wo_ref[...], preferred_element_type=f32)
        xo_ref[rs, :] = xo
        h = xo * lax.rsqrt(jnp.mean(xo * xo, axis=-1, keepdims=True) + NORM_EPS) * gf_ref[...]
        h_hi = h.astype(bf16)
        h_lo = (h - h_hi.astype(f32)).astype(bf16)
        h_ref[rs, :] = h_hi
        lg = lax.dot_general(rh_ref[...], h_hi, nt, preferred_element_type=f32)
        lg = lg + lax.dot_general(rh_ref[...], h_lo, nt, preferred_element_type=f32)
        lg = lg + lax.dot_general(rl_ref[...], h_hi, nt, preferred_element_type=f32)
        lg_ref[:, rs] = lg


def _post(lidx, ya, ys, z, gr, x, p):
    t = x.shape[0]

    def wspec(shape):
        return pl.BlockSpec((None,) + shape, lambda i, l: (l[0],) + (0,) * len(shape))

    def rows(width):
        return pl.BlockSpec((TM5, width), lambda i, l: (i, 0))

    grid_spec = pltpu.PrefetchScalarGridSpec(
        num_scalar_prefetch=1,
        grid=(t // TM5,),
        in_specs=[rows(LRU_WIDTH), rows(SSD_INNER), rows(SSD_INNER), rows(2 * D_MODEL), rows(D_MODEL),
                  wspec((LRU_WIDTH, D_MODEL)), wspec((SSD_INNER, D_MODEL)), wspec((D_MODEL, D_MODEL)),
                  wspec((1, SSD_INNER)), wspec((1, D_MODEL)),
                  wspec((N_EXPERTS, D_MODEL)), wspec((N_EXPERTS, D_MODEL))],
        out_specs=[rows(D_MODEL), rows(D_MODEL), pl.BlockSpec((N_EXPERTS, TM5), lambda i, l: (0, i))],
    )
    return pl.pallas_call(
        _post_kernel,
        grid_spec=grid_spec,
        out_shape=[jax.ShapeDtypeStruct((t, D_MODEL), f32),
                   jax.ShapeDtypeStruct((t, D_MODEL), bf16),
                   jax.ShapeDtypeStruct((N_EXPERTS, t), f32)],
        compiler_params=_cparams(("parallel",), 48 * 1024 * 1024),
        name="post",
    )(lidx, ya, ys, z, gr, x, p["w_br_a"], p["w_br_b"], p["w_out"], p["ssd_norm"], p["norm_ffn"],
      p["w_rt_hi"], p["w_rt_lo"])


def _route_kernel(lg_ref, aff_ref, pos_ref, excl_ref, bits_scr, *, cap):
    n = lg_ref.shape[1]
    lg = lg_ref[...]
    ex = jnp.exp(lg - jnp.max(lg, axis=0, keepdims=True))
    aff = ex / jnp.sum(ex, axis=0, keepdims=True)
    aff_ref[...] = aff
    bits_scr[...] = lax.bitcast_convert_type(aff, i32)

    def body(t, prefix):
        cand = prefix | jnp.left_shift(jnp.int32(1), 30 - t)
        cnt = jnp.sum((bits_scr[...] >= cand).astype(f32), axis=1, keepdims=True)
        return jnp.where(cnt >= cap, cand, prefix)

    thr = lax.fori_loop(0, 31, body, jnp.zeros((N_EXPERTS, 1), i32))
    bits = bits_scr[...]
    gt = (bits > thr).astype(f32)
    eq = (bits == thr).astype(f32)
    need = cap - jnp.sum(gt, axis=1, keepdims=True)

    ri = lax.broadcasted_iota(i32, (PFX, PFX), 0)
    ci = lax.broadcasted_iota(i32, (PFX, PFX), 1)
    upper = (ri <= ci).astype(bf16)

    def prefix_count(mask):
        outs = []
        carry = jnp.zeros((N_EXPERTS, 1), f32)
        for k in range(n // PFX):
            blk = mask[:, k * PFX:(k + 1) * PFX].astype(bf16)
            inc = jnp.dot(blk, upper, preferred_element_type=f32) + carry
            carry = inc[:, PFX - 1:PFX]
            outs.append(inc)
        return jnp.concatenate(outs, axis=1)

    sel = gt + eq * (prefix_count(eq) <= need).astype(f32)
    excl = prefix_count(sel) - sel
    excl_ref[...] = excl.astype(i32)
    pos_ref[...] = jnp.where(sel > 0.5, excl, -1.0).astype(i32)


def _route(logits_t, cap):
    n = logits_t.shape[1]
    kern = functools.partial(_route_kernel, cap=cap)
    full = pl.BlockSpec((N_EXPERTS, n), lambda i: (0, 0))
    return pl.pallas_call(
        kern,
        grid=(1,),
        in_specs=[full],
        out_specs=[full, full, full],
        out_shape=[jax.ShapeDtypeStruct((N_EXPERTS, n), f32),
                   jax.ShapeDtypeStruct((N_EXPERTS, n), i32),
                   jax.ShapeDtypeStruct((N_EXPERTS, n), i32)],
        scratch_shapes=[pltpu.VMEM((N_EXPERTS, n), i32)],
        compiler_params=_cparams(("arbitrary",), VMEM_LIMIT),
        name="route",
    )(logits_t)


def _slab_base(c0, cap):
    return jnp.minimum((c0 // BF16_ROWS) * BF16_ROWS, cap - SLAB)


def _dispatch_kernel(tbl_ref, pos_ref, h_ref, xe_ref, *, ntile, cap):
    k = pl.program_id(1)

    @pl.when(k == 0)
    def _():
        xe_ref[...] = jnp.zeros_like(xe_ref)

    experts = [pl.program_id(0) * DISP_EXPERTS + ee for ee in range(DISP_EXPERTS)]
    pos = [pos_ref[pl.ds(e, 1), :] for e in experts]
    bases = [[pl.multiple_of(_slab_base(tbl_ref[e * ntile + k * DISP_TILES + s], cap), BF16_ROWS)
              for s in range(DISP_TILES)] for e in experts]

    def place(rows):
        ji = lax.broadcasted_iota(i32, (rows, TOK_TILE), 0)
        for s in range(DISP_TILES):
            hs = h_ref[s * TOK_TILE:(s + 1) * TOK_TILE, :]
            for ee in range(DISP_EXPERTS):
                b = bases[ee][s]
                onehot = (ji == pos[ee][:, s * TOK_TILE:(s + 1) * TOK_TILE] - b).astype(bf16)
                slab = jnp.dot(onehot, hs, preferred_element_type=f32)
                xe_ref[ee, pl.ds(b, rows), :] = xe_ref[ee, pl.ds(b, rows), :] + slab.astype(bf16)

    few = None
    for ee, e in enumerate(experts):
        for s in range(DISP_TILES):
            t = k * DISP_TILES + s
            end = jnp.where(t + 1 < ntile, tbl_ref[e * ntile + jnp.minimum(t + 1, ntile - 1)], cap)
            ok = end - bases[ee][s] <= SLAB_FEW
            few = ok if few is None else jnp.logical_and(few, ok)

    @pl.when(few)
    def _():
        place(SLAB_FEW)

    @pl.when(jnp.logical_not(few))
    def _():
        place(SLAB)


def _dispatch(tbl, pos, h, cap):
    n = h.shape[0]
    ntile = n // TOK_TILE
    kern = functools.partial(_dispatch_kernel, ntile=ntile, cap=cap)
    grid_spec = pltpu.PrefetchScalarGridSpec(
        num_scalar_prefetch=1,
        grid=(N_EXPERTS // DISP_EXPERTS, ntile // DISP_TILES),
        in_specs=[pl.BlockSpec((N_EXPERTS, DISP_TILES * TOK_TILE), lambda e, k, t: (0, k)),
                  pl.BlockSpec((DISP_TILES * TOK_TILE, D_MODEL), lambda e, k, t: (k, 0))],
        out_specs=pl.BlockSpec((DISP_EXPERTS, cap, D_MODEL), lambda e, k, t: (e, 0, 0)),
    )
    return pl.pallas_call(
        kern,
        grid_spec=grid_spec,
        out_shape=jax.ShapeDtypeStruct((N_EXPERTS, cap, D_MODEL), bf16),
        compiler_params=_cparams(("parallel", "arbitrary"), 48 * 1024 * 1024),
        name="dispatch",
    )(tbl, pos, h)


def _ffn_kernel(l_ref, x_ref, wg_ref, wu_ref, wd_ref, o_ref, acc_scr):
    f = pl.program_id(2)

    @pl.when(f == 0)
    def _():
        acc_scr[...] = jnp.zeros_like(acc_scr)

    x = x_ref[...]
    g = jnp.dot(x, wg_ref[...].astype(bf16), preferred_element_type=f32)
    u = jnp.dot(x, wu_ref[...].astype(bf16), preferred_element_type=f32)
    hid = (g * _sigmoid(g) * u).astype(bf16)
    acc_scr[...] += jnp.dot(hid, wd_ref[...].astype(bf16), preferred_element_type=f32)

    @pl.when(f == pl.num_programs(2) - 1)
    def _():
        o_ref[...] = acc_scr[...].astype(bf16)


def _ffn(lidx, xe, p, cap):
    tmf = min(TMF, cap)
    grid_spec = pltpu.PrefetchScalarGridSpec(
        num_scalar_prefetch=1,
        grid=(N_EXPERTS, cap // tmf, EXPERT_FF // TFF),
        in_specs=[pl.BlockSpec((None, tmf, D_MODEL), lambda e, r, f, l: (e, r, 0)),
                  pl.BlockSpec((None, None, D_MODEL, TFF), lambda e, r, f, l: (l[0], e, 0, f)),
                  pl.BlockSpec((None, None, D_MODEL, TFF), lambda e, r, f, l: (l[0], e, 0, f)),
                  pl.BlockSpec((None, None, TFF, D_MODEL), lambda e, r, f, l: (l[0], e, f, 0))],
        out_specs=pl.BlockSpec((None, tmf, D_MODEL), lambda e, r, f, l: (e, r, 0)),
        scratch_shapes=[pltpu.VMEM((tmf, D_MODEL), f32)],
    )
    return pl.pallas_call(
        _ffn_kernel,
        grid_spec=grid_spec,
        out_shape=jax.ShapeDtypeStruct((N_EXPERTS, cap, D_MODEL), bf16),
        compiler_params=_cparams(("parallel", "parallel", "arbitrary"), 48 * 1024 * 1024),
        name="ffn",
    )(lidx, xe, p["w_gate"], p["w_up"], p["w_down"])


def _combine_kernel(tbl_ref, x_ref, pos_ref, aff_ref, *rest, ntile, cap):
    slabs = rest[:N_EXPERTS]
    o_ref = rest[N_EXPERTS]
    k = pl.program_id(0)
    acc = x_ref[...]
    pos = pos_ref[...]
    aff = aff_ref[...]
    ji = lax.broadcasted_iota(i32, (TOK_TILE, SLAB), 1)
    for e in range(N_EXPERTS):
        base = _slab_base(tbl_ref[e * ntile + k], cap)
        onehot = (ji == pos[:, e:e + 1] - base).astype(bf16)
        acc = acc + aff[:, e:e + 1] * jnp.dot(onehot, slabs[e][0], preferred_element_type=f32)
    o_ref[...] = acc


def _slab_spec(e, ntile, cap):
    def imap(k, t):
        return (e, pl.multiple_of(_slab_base(t[e * ntile + k], cap), BF16_ROWS), 0)
    return pl.BlockSpec((pl.Element(1), pl.Element(SLAB), pl.Element(D_MODEL)), imap)


def _combine(tbl, x, pos_t, aff_t, ye):
    n = x.shape[0]
    ntile = n // TOK_TILE
    cap = ye.shape[1]
    kern = functools.partial(_combine_kernel, ntile=ntile, cap=cap)
    grid_spec = pltpu.PrefetchScalarGridSpec(
        num_scalar_prefetch=1,
        grid=(ntile,),
        in_specs=[pl.BlockSpec((TOK_TILE, D_MODEL), lambda k, t: (k, 0)),
                  pl.BlockSpec((TOK_TILE, N_EXPERTS), lambda k, t: (k, 0)),
                  pl.BlockSpec((TOK_TILE, N_EXPERTS), lambda k, t: (k, 0))]
        + [_slab_spec(e, ntile, cap) for e in range(N_EXPERTS)],
        out_specs=pl.BlockSpec((TOK_TILE, D_MODEL), lambda k, t: (k, 0)),
    )
    return pl.pallas_call(
        kern,
        grid_spec=grid_spec,
        out_shape=jax.ShapeDtypeStruct((n, D_MODEL), f32),
        compiler_params=_cparams(("parallel",), 40 * 1024 * 1024),
        name="combine",
    )(tbl, x, pos_t, aff_t, *([ye] * N_EXPERTS))


def _norm_kernel(x_ref, g_ref, o_ref):
    x = x_ref[...]
    o_ref[...] = x * lax.rsqrt(jnp.mean(x * x, axis=-1, keepdims=True) + NORM_EPS) * g_ref[...]


def _final_norm(x, g):
    t = x.shape[0]
    return pl.pallas_call(
        _norm_kernel,
        grid=(t // TM1,),
        in_specs=[pl.BlockSpec((TM1, D_MODEL), lambda i: (i, 0)), pl.BlockSpec((1, D_MODEL), lambda i: (0, 0))],
        out_specs=pl.BlockSpec((TM1, D_MODEL), lambda i: (i, 0)),
        out_shape=jax.ShapeDtypeStruct((t, D_MODEL), f32),
        compiler_params=_cparams(("parallel",)),
        name="final_norm",
    )(x, g)


def _block_diag_pairs(w):
    d = w.shape[0]
    w = w.reshape(d, LRU_BLOCKS // 2, 2, LRU_BLOCK, LRU_BLOCK)
    zero = jnp.zeros_like(w[:, :, 0])
    top = jnp.concatenate([w[:, :, 0], zero], axis=-1)
    bot = jnp.concatenate([zero, w[:, :, 1]], axis=-1)
    return jnp.concatenate([top, bot], axis=-2)


def _pair_rows(v):
    d = v.shape[0]
    rest = v.shape[3:]
    v = jnp.moveaxis(v.reshape((d, 2, N_PAIR, 2) + rest), 1, 2).reshape((d, N_PAIR, 4) + rest)
    return jnp.concatenate([v, jnp.zeros_like(v)], axis=2).reshape((d, DT_ROWS) + rest)


def _prepare(norm_mix, w_in, lru_conv_w, lru_conv_b, lru_wa, lru_ba, lru_wx, lru_bx, lru_lambda,
             ssd_conv_w, ssd_conv_b, ssd_dt_bias, ssd_a_log, ssd_d, ssd_norm, w_branch, w_out,
             norm_ffn, w_router, w_gate, w_up, w_down):
    c_dt = 2 * LRU_WIDTH + SSD_INNER + SSD_XBC
    n_dt = 2 * SSD_HEADS
    w_main = jnp.concatenate([w_in[:, :, :c_dt], w_in[:, :, c_dt + n_dt:]], axis=-1).astype(bf16)
    w_dt = jnp.swapaxes(w_in[:, :, c_dt:c_dt + n_dt], 1, 2).reshape(DEPTH, 2, SSD_HEADS, D_MODEL)
    w_dt = _pair_rows(w_dt).astype(bf16)

    ident = jnp.zeros((DEPTH, CONV_K, 1), f32)
    conv_w = jnp.concatenate([
        lru_conv_w, jnp.broadcast_to(ident, (DEPTH, CONV_K, LRU_WIDTH + SSD_INNER)),
        ssd_conv_w, jnp.broadcast_to(ident, (DEPTH, CONV_K, 2 * D_MODEL))], axis=-1)
    conv_b = jnp.concatenate([
        lru_conv_b, jnp.zeros((DEPTH, LRU_WIDTH + SSD_INNER), f32),
        ssd_conv_b, jnp.zeros((DEPTH, 2 * D_MODEL), f32)], axis=-1)[:, None, :]

    lru_w = jnp.concatenate([_block_diag_pairs(lru_wa[:, 0]), _block_diag_pairs(lru_wx[:, 0]),
                             _block_diag_pairs(lru_wa[:, 1]), _block_diag_pairs(lru_wx[:, 1])], axis=-1).astype(bf16)
    ncb = LRU_WIDTH // LANES

    def chan(v):
        return v.reshape(DEPTH, ncb, 1, LANES)

    lru_b = jnp.concatenate([chan(lru_ba[:, 0]), chan(lru_bx[:, 0]), chan(lru_ba[:, 1]), chan(lru_bx[:, 1])], axis=-1)
    lru_lam = lru_lambda.reshape(DEPTH, 2, ncb, LANES).transpose(0, 2, 1, 3)

    return dict(
        norm_mix=norm_mix[:, None, :],
        w_main=w_main, w_dt=w_dt, b_dt=_pair_rows(ssd_dt_bias)[:, :, None],
        conv_w=conv_w, conv_b=conv_b,
        lru_w=lru_w, lru_b=lru_b, lru_lam=lru_lam,
        ssd_alog=_pair_rows(ssd_a_log).reshape(DEPTH, N_PAIR, 8, 1),
        ssd_dskip=jnp.repeat(ssd_d, SSD_HEAD_DIM, axis=-1).reshape(DEPTH, N_PAIR, 1, LANES),
        w_br_a=w_branch[:, :LRU_WIDTH].astype(bf16), w_br_b=w_branch[:, LRU_WIDTH:].astype(bf16),
        w_out=w_out.astype(bf16),
        ssd_norm=ssd_norm[:, None, :], norm_ffn=norm_ffn[:, None, :],
        w_rt_hi=_split_hi(jnp.swapaxes(w_router, 1, 2)), w_rt_lo=_split_lo(jnp.swapaxes(w_router, 1, 2)),
        w_gate=w_gate, w_up=w_up, w_down=w_down,
    )


def _split_hi(w):
    return w.astype(bf16)


def _split_lo(w):
    return (w - w.astype(bf16).astype(f32)).astype(bf16)


def _layer(lidx, x, p, seq_len):
    n = x.shape[0]
    cap = CAPACITY_FACTOR * n // N_EXPERTS
    u, lg, z, xbc, gr, dt = _inproj(lidx, x, p, seq_len)
    ya = _lru(lidx, u, lg, p, seq_len)
    ys = _ssd(lidx, xbc, dt, p, seq_len)
    xo, h, logits_t = _post(lidx, ya, ys, z, gr, x, p)
    aff, pos, excl = _route(logits_t, cap)
    tbl = excl[:, ::TOK_TILE].reshape(-1)
    xe = _dispatch(tbl, pos, h, cap)
    ye = _ffn(lidx, xe, p, cap)
    return _combine(tbl, xo, pos.T, aff.T, ye)


def kernel(x_prompt, x_sample, norm_mix, w_in, lru_conv_w, lru_conv_b, lru_wa, lru_ba, lru_wx, lru_bx, lru_lambda, ssd_conv_w, ssd_conv_b, ssd_dt_bias, ssd_a_log, ssd_d, ssd_norm, w_branch, w_out, norm_ffn, w_router, w_gate, w_up, w_down, norm_final):
    p = _prepare(norm_mix, w_in, lru_conv_w, lru_conv_b, lru_wa, lru_ba, lru_wx, lru_bx, lru_lambda,
                 ssd_conv_w, ssd_conv_b, ssd_dt_bias, ssd_a_log, ssd_d, ssd_norm, w_branch, w_out,
                 norm_ffn, w_router, w_gate, w_up, w_down)
    bp, lp, _ = x_prompt.shape
    bs, ls, _ = x_sample.shape

    def body(i, xs):
        lidx = jnp.reshape(i, (1,)).astype(i32)
        return (_layer(lidx, xs[0], p, lp), _layer(lidx, xs[1], p, ls))

    xp, xs = lax.fori_loop(0, DEPTH, body, (x_prompt.reshape(bp * lp, D_MODEL), x_sample.reshape(bs * ls, D_MODEL)))
    g = norm_final[None, :]
    return (_final_norm(xp, g).reshape(bp, lp, D_MODEL), _final_norm(xs, g).reshape(bs, ls, D_MODEL))
```
